```python
import jax, jax.numpy as jnp
from jax import lax
import numpy as np

D_MODEL = 1024
BATCH = 16
SEQ = 256
DEPTH = 4
DEC_BATCH = 4
DEC_SEQ = 1024
PAST_LEN = 512

GRID_W = 64
N_MIXERS = 3
N_RWKV_LAYERS = (DEPTH + 2) // 3
N_POOL_LAYERS = (DEPTH + 1) // 3
N_NA_LAYERS = DEPTH // 3
N_MOD = 6
NORM_EPS = 1e-6
RW_HEAD_DIM = 64
RW_HEADS = D_MODEL // RW_HEAD_DIM
RW_DECAY_LORA = 64
RW_ICLR_LORA = 64
RW_GATE_LORA = 128
RW_GN_EPS = 64e-5
POOL_WINDOWS = (2, 4, 8, 16)
POOL_GROUP = D_MODEL // len(POOL_WINDOWS)
NA_HEAD_DIM = 64
NA_HEADS = D_MODEL // NA_HEAD_DIM
NA_WIN_ROWS = 8
NA_WIN_COLS = 16
NA_QBLOCK_W = 16
NA_KBLOCK_W = NA_QBLOCK_W + NA_WIN_COLS
CTX_Q_BLOCK = 128
N_EXPERTS = 32
TOP_K = 4
D_EXPERT = D_MODEL
SWIGLU_LIMIT = 7.0
SWIGLU_ALPHA = 1.702

kernel_name = 'hybrid_rwkv7_pool_natten_moe_flow_step'


def rms_norm(x, gain):
    xf = x.astype(jnp.float32)
    y = xf * lax.rsqrt(jnp.mean(xf * xf, axis=-1, keepdims=True) + NORM_EPS)
    return (y * gain.astype(jnp.float32)).astype(x.dtype)


def adaln(cond, w_mod, b_mod):
    m = jax.nn.silu(cond) @ w_mod + b_mod
    return jnp.split(m[:, None, :], N_MOD, axis=-1)


def modulate(h, shift, scale):
    return h * (1 + scale) + shift


def centred_shift(x):
    pad = jnp.zeros_like(x[:, :1])
    prev = jnp.concatenate([pad, x[:, :-1]], axis=1)
    nxt = jnp.concatenate([x[:, 1:], pad], axis=1)
    return 0.5 * (prev + nxt)


def wkv7_scan(s0, r, decay, k, v, kk, a, reverse):
    def step(S, inp):
        r_t, w_t, k_t, v_t, kk_t, a_t = inp
        sa = jnp.einsum('bhvk,bhk->bhv', S, kk_t)
        S = (S * w_t[:, :, None, :]
             - sa[..., None] * (kk_t * a_t)[:, :, None, :]
             + v_t[..., None] * k_t[:, :, None, :])
        return S, jnp.einsum('bhvk,bhk->bhv', S, r_t)
    xs = tuple(jnp.swapaxes(t, 0, 1) for t in (r, decay, k, v, kk, a))
    s_final, y = lax.scan(step, s0, xs, reverse=reverse)
    return s_final, jnp.swapaxes(y, 0, 1)


def rwkv7_time_mix(h, s0, mu, w_rkv, w_out, w0, w1, w2, a0, a1, a2, g1, g2,
                   k_k, k_a, r_k, lnx_w, lnx_b):
    B, T, D = h.shape
    H, N = RW_HEADS, RW_HEAD_DIM
    f32 = jnp.float32
    xx = centred_shift(h) - h
    xr, xw, xk, xv, xa, xg = (h + xx * mu[j] for j in range(6))
    r = xr @ w_rkv[0]
    k = xk @ w_rkv[1]
    v = xv @ w_rkv[2]
    g = jax.nn.sigmoid(xg @ g1) @ g2
    w_in = w0[:, None, None, :] + jnp.einsum('nbtl,nld->nbtd', jnp.tanh(jnp.einsum('btd,ndl->nbtl', xw, w1)), w2)
    log_w = -jax.nn.softplus(-w_in.astype(f32)) - 0.5
    decay = jnp.exp(-jnp.exp(log_w))
    a = jax.nn.sigmoid((a0[:, None, None, :] + jnp.einsum('nbtl,nld->nbtd', jnp.einsum('btd,ndl->nbtl', xa, a1), a2)).astype(f32))
    kk = (k * k_k).astype(f32).reshape(B, T, H, N)
    kk = kk / jnp.maximum(jnp.linalg.norm(kk, axis=-1, keepdims=True), 1e-12)
    k_dir = k.astype(f32)[None] * (1 + (a - 1) * k_a.astype(f32))
    heads = lambda z: z.reshape(z.shape[:-1] + (H, N))
    r_h = heads(r.astype(f32))
    v_h = heads(v.astype(f32))
    dec_h, a_h, k_h = heads(decay), heads(a), heads(k_dir)
    s0 = s0.astype(f32)
    s_f, y_f = wkv7_scan(s0[:, 0], r_h, dec_h[0], k_h[0], v_h, kk, a_h[0], reverse=False)
    s_b, y_b = wkv7_scan(s0[:, 1], r_h, dec_h[1], k_h[1], v_h, kk, a_h[1], reverse=True)
    y = y_f + y_b
    mean = jnp.mean(y, axis=-1, keepdims=True)
    var = jnp.mean(jnp.square(y - mean), axis=-1, keepdims=True)
    y = ((y - mean) * lax.rsqrt(var + RW_GN_EPS)).reshape(B, T, D) * lnx_w.astype(f32) + lnx_b.astype(f32)
    bonus = jnp.sum(r_h[None] * k_h * r_k.astype(f32), axis=-1, keepdims=True) * v_h[None]
    y = y + jnp.sum(bonus, axis=0).reshape(B, T, D)
    out = (y.astype(h.dtype) * g) @ w_out
    return out, jnp.stack([s_f, s_b], axis=1)


def multi_scale_pool_mix(h, w_pool, scale):
    B, T, D = h.shape
    n_g = len(POOL_WINDOWS)
    hg = h.astype(jnp.float32).reshape(B, T, n_g, POOL_GROUP)
    prefix = jnp.concatenate([jnp.zeros((B, 1, n_g, POOL_GROUP), jnp.float32), jnp.cumsum(hg, axis=1)], axis=1)
    half = np.array(POOL_WINDOWS) // 2
    t = np.arange(T)[:, None]
    lo = np.maximum(t - half, 0)
    hi = np.minimum(t + half - 1, T - 1)
    grp = np.arange(n_g)[None, :]
    win_sum = prefix[:, hi + 1, grp] - prefix[:, lo, grp]
    count = (hi - lo + 1).astype(np.float32)[None, :, :, None]
    pooled = (win_sum / count - hg).astype(h.dtype)
    mixed = jnp.einsum('btgc,gce->btge', pooled, w_pool).reshape(B, T, D)
    return mixed * scale


def na_qkv(h, w_qkv, q_norm, k_norm):
    B, T, _ = h.shape
    qkv = (h @ w_qkv).reshape(B, T, 3, NA_HEADS, NA_HEAD_DIM)
    return rms_norm(qkv[:, :, 0], q_norm), rms_norm(qkv[:, :, 1], k_norm), qkv[:, :, 2]


def context_attention(q, k, v):
    B, C, H, Dh = q.shape
    n_blk = C // CTX_Q_BLOCK
    qb = jnp.moveaxis(q.reshape(B, n_blk, CTX_Q_BLOCK, H, Dh), 1, 0)
    def block(q_blk):
        s = jnp.einsum('bqhd,bkhd->bhqk', q_blk, k).astype(jnp.float32) * (Dh ** -0.5)
        p = jax.nn.softmax(s, axis=-1).astype(v.dtype)
        return jnp.einsum('bhqk,bkhd->bqhd', p, v)
    o = lax.map(block, qb)
    return jnp.moveaxis(o, 0, 1).reshape(B, C, H, Dh)


def _na_column_layout():
    n_cb = GRID_W // NA_QBLOCK_W
    q_cols = np.arange(GRID_W).reshape(n_cb, NA_QBLOCK_W)
    kb_start = np.clip(np.arange(n_cb) * NA_QBLOCK_W - NA_WIN_COLS // 2, 0, GRID_W - NA_KBLOCK_W)
    k_cols = kb_start[:, None] + np.arange(NA_KBLOCK_W)
    q_start = np.clip(q_cols - NA_WIN_COLS // 2, 0, GRID_W - NA_WIN_COLS)
    col_in = (k_cols[:, None, :] >= q_start[..., None]) & (k_cols[:, None, :] < q_start[..., None] + NA_WIN_COLS)
    dc_idx = np.clip(k_cols[:, None, :] - q_cols[..., None] + NA_WIN_COLS - 1, 0, 2 * NA_WIN_COLS - 2)
    return n_cb, k_cols, col_in, dc_idx


def neighbourhood_attention_latent(q, k, v, ck, cv, rpb):
    B, L, H, Dh = q.shape
    rows = L // GRID_W
    win_r = min(NA_WIN_ROWS, rows)
    scale = Dh ** -0.5
    n_cb, k_cols, col_in, dc_idx = _na_column_layout()
    n_loc = win_r * NA_KBLOCK_W
    mask = np.broadcast_to(col_in[:, :, None, :], (n_cb, NA_QBLOCK_W, win_r, NA_KBLOCK_W)).reshape(n_cb, NA_QBLOCK_W, n_loc)
    qg = q.reshape(B, rows, GRID_W, H, Dh)
    kg = k.reshape(B, rows, GRID_W, H, Dh)
    vg = v.reshape(B, rows, GRID_W, H, Dh)

    def row_block(r):
        rs = jnp.clip(r - win_r // 2, 0, rows - win_r)
        q_r = lax.dynamic_index_in_dim(qg, r, axis=1, keepdims=False)
        k_r = lax.dynamic_slice_in_dim(kg, rs, win_r, axis=1)
        v_r = lax.dynamic_slice_in_dim(vg, rs, win_r, axis=1)
        kb = jnp.moveaxis(k_r[:, :, k_cols], 2, 1).reshape(B, n_cb, n_loc, H, Dh)
        vb = jnp.moveaxis(v_r[:, :, k_cols], 2, 1).reshape(B, n_cb, n_loc, H, Dh)
        qb = q_r.reshape(B, n_cb, NA_QBLOCK_W, H, Dh)
        s_loc = jnp.einsum('bnqhd,bnkhd->bhnqk', qb, kb).astype(jnp.float32) * scale
        dr_idx = rs + jnp.arange(win_r) - r + NA_WIN_ROWS - 1
        bias = rpb[:, dr_idx][:, :, dc_idx]
        bias = jnp.transpose(bias, (0, 2, 3, 1, 4)).reshape(H, n_cb, NA_QBLOCK_W, n_loc)
        s_loc = jnp.where(mask, s_loc + bias[None].astype(jnp.float32), -jnp.inf)
        s_ctx = jnp.einsum('bnqhd,bchd->bhnqc', qb, ck).astype(jnp.float32) * scale
        p = jax.nn.softmax(jnp.concatenate([s_loc, s_ctx], axis=-1), axis=-1).astype(v.dtype)
        o = (jnp.einsum('bhnqk,bnkhd->bnqhd', p[..., :n_loc], vb)
             + jnp.einsum('bhnqc,bchd->bnqhd', p[..., n_loc:], cv))
        return o.reshape(B, GRID_W, H, Dh)

    out = lax.map(row_block, jnp.arange(rows))
    return jnp.moveaxis(out, 0, 1).reshape(B, L, H, Dh)


def moe_clamped_swiglu(h, w_router, b_router, w_gu, b_gu, w_down, b_down):
    B, T, D = h.shape
    ht = h.reshape(B * T, D)
    logits = (ht @ w_router + b_router).astype(jnp.float32)
    top_val, top_idx = lax.top_k(logits, TOP_K)
    gates = jnp.sum(jax.nn.one_hot(top_idx, N_EXPERTS, dtype=jnp.float32) * jax.nn.softmax(top_val, axis=-1)[..., None], axis=1)
    gates = gates.astype(h.dtype)
    hid = jnp.einsum('td,edf->tef', ht, w_gu) + b_gu
    glu = jnp.minimum(hid[..., :D_EXPERT], SWIGLU_LIMIT)
    lin = jnp.clip(hid[..., D_EXPERT:], -SWIGLU_LIMIT, SWIGLU_LIMIT)
    act = glu * jax.nn.sigmoid(SWIGLU_ALPHA * glu) * (lin + 1) * gates[..., None]
    out = jnp.einsum('tef,efd->td', act, w_down) + gates @ b_down
    return out.reshape(B, T, D)


def setup_inputs(seed: int = 0) -> dict:
    key = jax.random.key(seed)
    ks = iter(jax.random.split(key, 48))
    def nrm(shape, scale):
        return jax.random.normal(next(ks), shape, jnp.float32) * scale
    def gain(shape):
        return 1.0 + nrm(shape, 0.02)
    D, F, E = D_MODEL, D_EXPERT, N_EXPERTS
    NR, NP, NN = N_RWKV_LAYERS, N_POOL_LAYERS, N_NA_LAYERS
    return {
        'x_prompt': nrm((BATCH, SEQ, D), 1.0),
        'x_sample': nrm((DEC_BATCH, DEC_SEQ, D), 1.0),
        'c': nrm((DEC_BATCH, D), 1.0),
        'c_ctx': nrm((D,), 1.0),
        'state_wkv': nrm((DEC_BATCH, NR, 2, RW_HEADS, RW_HEAD_DIM, RW_HEAD_DIM), 0.5),
        'cache_k': nrm((DEC_BATCH, NN, PAST_LEN, NA_HEADS, NA_HEAD_DIM), 1.0),
        'cache_v': nrm((DEC_BATCH, NN, PAST_LEN, NA_HEADS, NA_HEAD_DIM), 1.0),
        'norm_mix': gain((DEPTH, D)),
        'norm_ffn': gain((DEPTH, D)),
        'w_mod': nrm((DEPTH, D, N_MOD * D), 0.5 * D ** -0.5),
        'b_mod': nrm((DEPTH, N_MOD * D), 0.02),
        'rw_mu': jax.random.uniform(next(ks), (NR, 6, D), jnp.float32),
        'rw_w_rkv': nrm((NR, 3, D, D), D ** -0.5),
        'rw_w_out': nrm((NR, D, D), D ** -0.5),
        'rw_w0': jax.random.uniform(next(ks), (NR, 2, D), jnp.float32, -6.0, -1.0),
        'rw_w1': nrm((NR, 2, D, RW_DECAY_LORA), D ** -0.5),
        'rw_w2': nrm((NR, 2, RW_DECAY_LORA, D), 0.1 * RW_DECAY_LORA ** -0.5),
        'rw_a0': nrm((NR, 2, D), 0.1),
        'rw_a1': nrm((NR, 2, D, RW_ICLR_LORA), D ** -0.5),
        'rw_a2': nrm((NR, 2, RW_ICLR_LORA, D), 0.1 * RW_ICLR_LORA ** -0.5),
        'rw_g1': nrm((NR, D, RW_GATE_LORA), D ** -0.5),
        'rw_g2': nrm((NR, RW_GATE_LORA, D), RW_GATE_LORA ** -0.5),
        'rw_k_k': 0.85 + nrm((NR, D), 0.02),
        'rw_k_a': gain((NR, D)),
        'rw_r_k': nrm((NR, RW_HEADS, RW_HEAD_DIM), 0.1),
        'rw_lnx_w': gain((NR, D)),
        'rw_lnx_b': nrm((NR, D), 0.02),
        'pool_w': nrm((NP, len(POOL_WINDOWS), POOL_GROUP, POOL_GROUP), POOL_GROUP ** -0.5),
        'pool_scale': 1.0 + nrm((NP, D), 0.1),
        'na_w_qkv': nrm((NN, D, 3 * D), D ** -0.5),
        'na_w_out': nrm((NN, D, D), D ** -0.5),
        'na_q_norm': gain((NN, NA_HEAD_DIM)),
        'na_k_norm': gain((NN, NA_HEAD_DIM)),
        'na_rpb': nrm((NN, NA_HEADS, 2 * NA_WIN_ROWS - 1, 2 * NA_WIN_COLS - 1), 0.1),
        'moe_w_router': nrm((DEPTH, D, E), D ** -0.5),
        'moe_b_router': nrm((DEPTH, E), 0.01),
        'moe_w_gu': nrm((DEPTH, E, D, 2 * F), D ** -0.5),
        'moe_b_gu': nrm((DEPTH, E, 2 * F), 0.02),
        'moe_w_down': nrm((DEPTH, E, F, D), F ** -0.5),
        'moe_b_down': nrm((DEPTH, E, D), 0.02),
    }


def reference(x_prompt, x_sample, c, c_ctx, state_wkv, cache_k, cache_v,
              norm_mix, norm_ffn, w_mod, b_mod,
              rw_mu, rw_w_rkv, rw_w_out, rw_w0, rw_w1, rw_w2, rw_a0, rw_a1, rw_a2,
              rw_g1, rw_g2, rw_k_k, rw_k_a, rw_r_k, rw_lnx_w, rw_lnx_b,
              pool_w, pool_scale,
              na_w_qkv, na_w_out, na_q_norm, na_k_norm, na_rpb,
              moe_w_router, moe_b_router, moe_w_gu, moe_b_gu, moe_w_down, moe_b_down):
    xp, xs = x_prompt, x_sample
    new_wkv, new_k, new_v = [], [], []
    for i in range(DEPTH):
        kind, slot = i % N_MIXERS, i // N_MIXERS
        sh1p, sc1p, ga1p, sh2p, sc2p, ga2p = adaln(c_ctx[None, :], w_mod[i], b_mod[i])
        sh1s, sc1s, ga1s, sh2s, sc2s, ga2s = adaln(c, w_mod[i], b_mod[i])
        hp = modulate(rms_norm(xp, norm_mix[i]), sh1p, sc1p)
        hs = modulate(rms_norm(xs, norm_mix[i]), sh1s, sc1s)
        if kind == 0:
            rw = (rw_mu[slot], rw_w_rkv[slot], rw_w_out[slot], rw_w0[slot], rw_w1[slot], rw_w2[slot],
                  rw_a0[slot], rw_a1[slot], rw_a2[slot], rw_g1[slot], rw_g2[slot], rw_k_k[slot],
                  rw_k_a[slot], rw_r_k[slot], rw_lnx_w[slot], rw_lnx_b[slot])
            zero_state = jnp.zeros((xp.shape[0], 2, RW_HEADS, RW_HEAD_DIM, RW_HEAD_DIM), jnp.float32)
            mp, s_ctx = rwkv7_time_mix(hp, zero_state, *rw)
            ms, _ = rwkv7_time_mix(hs, state_wkv[:, slot], *rw)
            new_wkv.append(s_ctx)
        elif kind == 1:
            mp = multi_scale_pool_mix(hp, pool_w[slot], pool_scale[slot])
            ms = multi_scale_pool_mix(hs, pool_w[slot], pool_scale[slot])
        else:
            qp, kp, vp = na_qkv(hp, na_w_qkv[slot], na_q_norm[slot], na_k_norm[slot])
            qs, ks_, vs = na_qkv(hs, na_w_qkv[slot], na_q_norm[slot], na_k_norm[slot])
            op = context_attention(qp, kp, vp)
            o_s = neighbourhood_attention_latent(qs, ks_, vs, cache_k[:, slot], cache_v[:, slot], na_rpb[slot])
            mp = op.reshape(xp.shape) @ na_w_out[slot]
            ms = o_s.reshape(xs.shape) @ na_w_out[slot]
            new_k.append(kp)
            new_v.append(vp)
        xp = xp + ga1p * mp
        xs = xs + ga1s * ms
        moe = (moe_w_router[i], moe_b_router[i], moe_w_gu[i], moe_b_gu[i], moe_w_down[i], moe_b_down[i])
        xp = xp + ga2p * moe_clamped_swiglu(modulate(rms_norm(xp, norm_ffn[i]), sh2p, sc2p), *moe)
        xs = xs + ga2s * moe_clamped_swiglu(modulate(rms_norm(xs, norm_ffn[i]), sh2s, sc2s), *moe)
    return (xp, xs, jnp.stack(new_wkv, axis=1), jnp.stack(new_k, axis=1), jnp.stack(new_v, axis=1))
```

```python
import functools

import jax
import jax.numpy as jnp
import numpy as np
from jax import lax
from jax.experimental import pallas as pl
from jax.experimental.pallas import tpu as pltpu

F32 = jnp.float32
BF16 = jnp.bfloat16

D = 1024
BATCH, SEQ = 16, 256
DEC_BATCH, DEC_SEQ = 4, 1024
DEPTH = 4
PAST_LEN = 512
GRID_W = 64
N_MIXERS = 3
N_MOD = 6
NORM_EPS = 1e-6
HEAD = 64
HEADS = D // HEAD
RW_GN_EPS = 64e-5
POOL_WINDOWS = (2, 4, 8, 16)
POOL_GROUP = D // len(POOL_WINDOWS)
NA_WIN_ROWS, NA_WIN_COLS = 8, 16
N_EXPERTS, TOP_K = 32, 4
D_EXPERT = D
SWIGLU_LIMIT, SWIGLU_ALPHA = 7.0, 1.702

N_PROMPT = BATCH * SEQ
N_LATENT = DEC_BATCH * DEC_SEQ
N_TOK = N_PROMPT + N_LATENT
COND_ROWS = 8
ROW_BLOCK = 1024
PROMPT_BLOCKS = N_PROMPT // ROW_BLOCK
LANES = 128
HEADS_PER_STEP = LANES // HEAD
CHUNK = 64
MOE_TILE = 256
MOE_ROWS = N_TOK * TOP_K + N_EXPERTS * MOE_TILE
MOE_TILES = MOE_ROWS // MOE_TILE
COMBINE_TILE = 128
NEG_BIG = -1e30
VMEM_LIMIT = 56 * 1024 * 1024


def _cond_row(block_1024):
    return jnp.maximum(block_1024 - (PROMPT_BLOCKS - 1), 0)


def _mods_spec(rows_per_block):
    per = ROW_BLOCK // rows_per_block
    return pl.BlockSpec((None, N_MOD, 1, D), lambda *ids: (_cond_row(ids[-1] // per), 0, 0, 0))


def _params(sem):
    return pltpu.CompilerParams(dimension_semantics=sem, vmem_limit_bytes=VMEM_LIMIT)


def _bdot(a, b):
    return jnp.dot(a.astype(BF16), b.astype(BF16), preferred_element_type=F32)


def _bdot_nt(a, b):
    return lax.dot_general(a.astype(BF16), b.astype(BF16), (((1,), (1,)), ((), ())),
                           preferred_element_type=F32)


def _bdot_tn(a, b):
    return lax.dot_general(a.astype(BF16), b.astype(BF16), (((0,), (0,)), ((), ())),
                           preferred_element_type=F32)


def _split3(x):
    hi = x.astype(BF16)
    r1 = x - hi.astype(F32)
    mid = r1.astype(BF16)
    lo = (r1 - mid.astype(F32)).astype(BF16)
    return hi, mid, lo


def _sigmoid(x):
    return 1.0 / (1.0 + jnp.exp(-x))


def _rms_mod(x, gain, shift, scale):
    y = x * lax.rsqrt(jnp.mean(x * x, axis=-1, keepdims=True) + NORM_EPS)
    return (y * gain) * (1.0 + scale) + shift


def _adaln_kernel(c_ref, w_ref, b_ref, o_ref):
    c = c_ref[...]
    s = c * _sigmoid(c)
    s_hi = s.astype(BF16)
    s_lo = (s - s_hi.astype(F32)).astype(BF16)
    w = w_ref[...].astype(BF16)
    o_ref[...] = (jnp.dot(s_hi, w, preferred_element_type=F32)
                  + jnp.dot(s_lo, w, preferred_element_type=F32) + b_ref[...])


def _adaln(cond, w_mod, b_mod):
    out = pl.pallas_call(
        _adaln_kernel,
        out_shape=jax.ShapeDtypeStruct((DEPTH, COND_ROWS, N_MOD * D), F32),
        grid=(DEPTH, N_MOD),
        in_specs=[pl.BlockSpec((COND_ROWS, D), lambda l, j: (0, 0)),
                  pl.BlockSpec((None, D, D), lambda l, j: (l, 0, j)),
                  pl.BlockSpec((None, 1, D), lambda l, j: (l, 0, j))],
        out_specs=pl.BlockSpec((None, COND_ROWS, D), lambda l, j: (l, 0, j)),
        compiler_params=_params(("parallel", "parallel")),
        name="adaln",
    )(cond, w_mod, b_mod.reshape(DEPTH, 1, N_MOD * D))
    return out.reshape(DEPTH, COND_ROWS, N_MOD, 1, D)


def _normmod_kernel(x_ref, g_ref, m_ref, h_ref, *xx_ref, shift_idx, scale_idx):
    h = _rms_mod(x_ref[...], g_ref[...], m_ref[shift_idx], m_ref[scale_idx])
    h_ref[...] = h
    if xx_ref:
        seq = jnp.where(pl.program_id(0) < PROMPT_BLOCKS, SEQ, DEC_SEQ)
        t = lax.broadcasted_iota(jnp.int32, (ROW_BLOCK, 1), 0) & (seq - 1)
        prev = jnp.where(t == 0, 0.0, pltpu.roll(h, 1, 0))
        nxt = jnp.where(t == seq - 1, 0.0, pltpu.roll(h, ROW_BLOCK - 1, 0))
        xx_ref[0][...] = 0.5 * (prev + nxt) - h


def _normmod(x, gain, mods, with_xx):
    n_out = 2 if with_xx else 1
    row = pl.BlockSpec((ROW_BLOCK, D), lambda i: (i, 0))
    outs = pl.pallas_call(
        functools.partial(_normmod_kernel, shift_idx=0, scale_idx=1),
        out_shape=[jax.ShapeDtypeStruct((N_TOK, D), F32)] * n_out,
        grid=(N_TOK // ROW_BLOCK,),
        in_specs=[row, pl.BlockSpec((1, D), lambda i: (0, 0)), _mods_spec(ROW_BLOCK)],
        out_specs=[row] * n_out,
        compiler_params=_params(("parallel",)),
        name="normmod",
    )(x, gain.reshape(1, D), mods)
    return outs


PROJ_TILE = 512


def _proj3_kernel(*refs, mix):
    if mix:
        h_ref, xx_ref, mu_ref, w_ref, o_ref, wbf_ref = refs
    else:
        h_ref, w_ref, o_ref, wbf_ref = refs

    @pl.when(pl.program_id(1) == 0)
    def _():
        wbf_ref[...] = w_ref[...].astype(BF16)

    x = h_ref[...]
    if mix:
        x = x + xx_ref[...] * mu_ref[...]
    o_ref[...] = jnp.dot(x.astype(BF16), wbf_ref[...], preferred_element_type=F32)


def _proj3(h, w, xx=None, mu=None):
    mix = xx is not None
    row = pl.BlockSpec((PROJ_TILE, D), lambda j, i: (i, 0))
    if mix:
        ins = [h, xx, mu, w]
        specs = [row, row, pl.BlockSpec((None, 1, D), lambda j, i: (j, 0, 0)),
                 pl.BlockSpec((None, D, D), lambda j, i: (j, 0, 0))]
    else:
        ins = [h, w]
        specs = [row, pl.BlockSpec((D, D), lambda j, i: (0, j))]
    return pl.pallas_call(
        functools.partial(_proj3_kernel, mix=mix),
        out_shape=jax.ShapeDtypeStruct((3, N_TOK, D), F32),
        grid=(3, N_TOK // PROJ_TILE),
        in_specs=specs,
        out_specs=pl.BlockSpec((None, PROJ_TILE, D), lambda j, i: (j, i, 0)),
        scratch_shapes=[pltpu.VMEM((D, D), BF16)],
        compiler_params=_params(("arbitrary", "arbitrary")),
        name="proj3",
    )(*ins)


def _lora_kernel(h_ref, xx_ref, mu_ref, a_ref, b_ref, *rest, n, act, has_bias):
    if has_bias:
        bias_ref, o_ref = rest
    else:
        (o_ref,) = rest
    h = h_ref[...]
    xx = xx_ref[...]
    for j in range(n):
        x = h + xx * mu_ref[j]
        t = _bdot(x, a_ref[j])
        if act == "tanh":
            t = jnp.tanh(t)
        elif act == "sigmoid":
            t = _sigmoid(t)
        o = _bdot(t, b_ref[j])
        if has_bias:
            o = o + bias_ref[j]
        o_ref[j] = o


def _lora(h, xx, mu, a, b, bias, act):
    n, _, r = a.shape
    row = pl.BlockSpec((PROJ_TILE, D), lambda i: (i, 0))
    full = lambda shape: pl.BlockSpec(shape, lambda i: (0,) * len(shape))
    ins = [h, xx, mu, a, b]
    specs = [row, row, full((n, 1, D)), full((n, D, r)), full((n, r, D))]
    if bias is not None:
        ins.append(bias)
        specs.append(full((n, 1, D)))
    return pl.pallas_call(
        functools.partial(_lora_kernel, n=n, act=act, has_bias=bias is not None),
        out_shape=jax.ShapeDtypeStruct((n, N_TOK, D), F32),
        grid=(N_TOK // PROJ_TILE,),
        in_specs=specs,
        out_specs=pl.BlockSpec((n, PROJ_TILE, D), lambda i: (0, i, 0)),
        compiler_params=_params(("parallel",)),
        name="lora_" + act,
    )(*ins)


def _mmres_kernel(*refs, with_g, gate_idx):
    if with_g:
        z_ref, g_ref, w_ref, x_ref, m_ref, o_ref, wbf_ref = refs
    else:
        z_ref, w_ref, x_ref, m_ref, o_ref, wbf_ref = refs

    @pl.when(pl.program_id(0) == 0)
    def _():
        wbf_ref[...] = w_ref[...].astype(BF16)

    z = z_ref[...]
    if with_g:
        z = z * g_ref[...]
    o_ref[...] = x_ref[...] + m_ref[gate_idx] * jnp.dot(z.astype(BF16), wbf_ref[...],
                                                        preferred_element_type=F32)


def _mmres(z, w, x, mods, g=None):
    row = pl.BlockSpec((PROJ_TILE, D), lambda i: (i, 0))
    ins, specs = [z], [row]
    if g is not None:
        ins.append(g)
        specs.append(row)
    ins += [w, x, mods]
    specs += [pl.BlockSpec((D, D), lambda i: (0, 0)), row, _mods_spec(PROJ_TILE)]
    return pl.pallas_call(
        functools.partial(_mmres_kernel, with_g=g is not None, gate_idx=2),
        out_shape=jax.ShapeDtypeStruct((N_TOK, D), F32),
        grid=(N_TOK // PROJ_TILE,),
        in_specs=specs,
        out_specs=row,
        scratch_shapes=[pltpu.VMEM((D, D), BF16)],
        compiler_params=_params(("arbitrary",)),
        name="mmres",
    )(*ins)


def _inv_unit_triangular(tri, eye, blk16, blk32):
    d16 = jnp.where(blk16, tri, 0.0)
    x = eye - d16
    p = _bdot(d16, d16)
    x = x + _bdot(x, p)
    p = _bdot(p, p)
    x = x + _bdot(x, p)
    p = _bdot(p, p)
    x = x + _bdot(x, p)
    off = jnp.where(blk32 & (~blk16), tri, 0.0)
    x = x - _bdot(_bdot(x, off), x)
    off = jnp.where(~blk32, tri, 0.0)
    x = x - _bdot(_bdot(x, off), x)
    return x


def _wkv_kernel(*refs, seq, has_s0, emit_state):
    it = iter(refs)
    rkv_ref, wl_ref, al_ref = next(it), next(it), next(it)
    s0_ref = next(it) if has_s0 else None
    kk_ref, ka_ref, rk_ref, lw_ref, lb_ref = next(it), next(it), next(it), next(it), next(it)
    z_ref = next(it)
    sf_ref = next(it) if emit_state else None
    s_ref, y_ref = next(it), next(it)
    n_chunks = seq // CHUNK

    if has_s0:
        s_ref[...] = s0_ref[...]
    else:
        s_ref[...] = jnp.zeros_like(s_ref)

    ri = lax.broadcasted_iota(jnp.int32, (CHUNK, CHUNK), 0)
    ci = lax.broadcasted_iota(jnp.int32, (CHUNK, CHUNK), 1)
    eye = (ri == ci).astype(F32)
    blk16 = (ri // 16) == (ci // 16)
    blk32 = (ri // 32) == (ci // 32)
    incl = (ri >= ci, ri <= ci)
    strict = (ri > ci, ri < ci)
    k_k, k_a, r_k = kk_ref[...], ka_ref[...], rk_ref[...]

    def chunk_step(c, carry):
        for d in range(2):
            cc = c if d == 0 else n_chunks - 1 - c
            rows = pl.ds(pl.multiple_of(cc * CHUNK, CHUNK), CHUNK)
            r2, k2, v2 = rkv_ref[0, rows, :], rkv_ref[1, rows, :], rkv_ref[2, rows, :]
            w_in = -wl_ref[d, rows, :]
            softplus = jnp.maximum(w_in, 0.0) + jnp.log(1.0 + jnp.exp(-jnp.abs(w_in)))
            logdec = -jnp.exp(-softplus - 0.5)
            a2 = _sigmoid(al_ref[d, rows, :])
            kk2 = k2 * k_k
            kd2 = k2 * (1.0 + (a2 - 1.0) * k_a)
            tri_bf = incl[d].astype(BF16)
            mask2 = jnp.concatenate([strict[d], incl[d]], axis=0)
            ys = []
            for hh in range(HEADS_PER_STEP):
                sl = slice(hh * HEAD, (hh + 1) * HEAD)
                r, v, kd, a, lw = r2[:, sl], v2[:, sl], kd2[:, sl], a2[:, sl], logdec[:, sl]
                kk = kk2[:, sl]
                kk = kk / jnp.maximum(jnp.sqrt(jnp.sum(kk * kk, axis=-1, keepdims=True)), 1e-12)
                b = kk * a
                hi, mid, lo = _split3(lw)
                lp = (jnp.dot(tri_bf, hi, preferred_element_type=F32)
                      + jnp.dot(tri_bf, mid, preferred_element_type=F32)
                      + jnp.dot(tri_bf, lo, preferred_element_type=F32))
                lp_end = lp[CHUNK - 1:CHUNK, :] if d == 0 else lp[0:1, :]
                e_pos, e_neg, p_end = jnp.exp(lp), jnp.exp(-lp), jnp.exp(lp_end)
                kt = kk * jnp.exp(lp - lw)
                rt = r * e_pos
                ks = kd * e_neg
                bs = b * e_neg
                s = s_ref[d, hh]
                q2 = jnp.concatenate([kt, rt], axis=0)
                a_k = jnp.where(mask2, _bdot_nt(q2, ks), 0.0)
                a_b = jnp.where(mask2, _bdot_nt(q2, bs), 0.0)
                qs = _bdot_nt(q2, s)
                av = _bdot(a_k, v)
                u = _bdot(_inv_unit_triangular(a_b[:CHUNK], eye, blk16, blk32),
                          qs[:CHUNK] + av[:CHUNK])
                ys.append(qs[CHUNK:] + av[CHUNK:] - _bdot(a_b[CHUNK:], u))
                vu = jnp.concatenate([v, -u], axis=0)
                kb = jnp.concatenate([ks * p_end, bs * p_end], axis=0)
                s_ref[d, hh] = s * p_end + _bdot_tn(vu, kb)
            y_ref[d, rows, :] = jnp.concatenate(ys, axis=1)
        return carry

    lax.fori_loop(0, n_chunks, chunk_step, 0)

    if emit_state:
        sf_ref[...] = s_ref[...]

    EP = 256
    lnx_w, lnx_b = lw_ref[...], lb_ref[...]

    def epilogue(i, carry):
        rows = pl.ds(pl.multiple_of(i * EP, EP), EP)
        r2, k2, v2 = rkv_ref[0, rows, :], rkv_ref[1, rows, :], rkv_ref[2, rows, :]
        y2 = y_ref[0, rows, :] + y_ref[1, rows, :]
        coef = 2.0 + (_sigmoid(al_ref[0, rows, :]) + _sigmoid(al_ref[1, rows, :]) - 2.0) * k_a
        rkr = r2 * k2 * coef * r_k
        outs = []
        for hh in range(HEADS_PER_STEP):
            sl = slice(hh * HEAD, (hh + 1) * HEAD)
            y = y2[:, sl]
            mean = jnp.mean(y, axis=-1, keepdims=True)
            var = jnp.mean(jnp.square(y - mean), axis=-1, keepdims=True)
            yn = (y - mean) * lax.rsqrt(var + RW_GN_EPS)
            bonus = jnp.sum(rkr[:, sl], axis=-1, keepdims=True) * v2[:, sl]
            outs.append((yn, bonus))
        yn = jnp.concatenate([o[0] for o in outs], axis=1)
        bonus = jnp.concatenate([o[1] for o in outs], axis=1)
        z_ref[rows, :] = yn * lnx_w + lnx_b + bonus
        return carry

    lax.fori_loop(0, seq // EP, epilogue, 0)


def _wkv(rkv, wl, al, s0, k_k, k_a, r_k, lnx_w, lnx_b, *, batch, seq, row_block0, emit_state):
    has_s0 = s0 is not None
    pair = HEADS // HEADS_PER_STEP
    tok = lambda lead: pl.BlockSpec((lead, seq, LANES), lambda b, p: (0, row_block0 + b, p))
    vec = pl.BlockSpec((1, LANES), lambda b, p: (0, p))
    st = pl.BlockSpec((None, 2, HEADS_PER_STEP, HEAD, HEAD), lambda b, p: (b, 0, p, 0, 0))
    ins, specs = [rkv, wl, al], [tok(3), tok(2), tok(2)]
    if has_s0:
        ins.append(s0)
        specs.append(st)
    ins += [k_k.reshape(1, D), k_a.reshape(1, D), r_k.reshape(1, D), lnx_w.reshape(1, D), lnx_b.reshape(1, D)]
    specs += [vec] * 5
    out_shape = [jax.ShapeDtypeStruct((batch * seq, D), F32)]
    out_specs = [pl.BlockSpec((seq, LANES), lambda b, p: (b, p))]
    if emit_state:
        out_shape.append(jax.ShapeDtypeStruct((batch, 2, HEADS, HEAD, HEAD), F32))
        out_specs.append(st)
    return pl.pallas_call(
        functools.partial(_wkv_kernel, seq=seq, has_s0=has_s0, emit_state=emit_state),
        out_shape=out_shape,
        grid=(batch, pair),
        in_specs=specs,
        out_specs=out_specs,
        scratch_shapes=[pltpu.VMEM((2, HEADS_PER_STEP, HEAD, HEAD), F32), pltpu.VMEM((2, seq, LANES), F32)],
        compiler_params=_params(("parallel", "parallel")),
        name="wkv_%d" % seq,
    )(*ins)


def _pool_kernel(x_ref, g_ref, m_ref, w_ref, sc_ref, o_ref):
    x = x_ref[...]
    h = _rms_mod(x, g_ref[...], m_ref[0], m_ref[1])
    seq = jnp.where(pl.program_id(0) < PROMPT_BLOCKS, SEQ, DEC_SEQ)
    t = lax.broadcasted_iota(jnp.int32, (ROW_BLOCK, 1), 0) & (seq - 1)
    gate = m_ref[2]
    scale = sc_ref[...]
    for g, win in enumerate(POOL_WINDOWS):
        half = win // 2
        cols = slice(g * POOL_GROUP, (g + 1) * POOL_GROUP)
        hg = h[:, cols]
        up = lambda z, m: jnp.where(t + m <= seq - 1, pltpu.roll(z, ROW_BLOCK - m, 0), 0.0)
        down = lambda z, m: jnp.where(t - m >= 0, pltpu.roll(z, m, 0), 0.0)
        fwd = hg
        bwd = down(hg, 1)
        m = 1
        while m < half:
            fwd = fwd + up(fwd, m)
            bwd = bwd + down(bwd, m)
            m *= 2
        count = (jnp.minimum(t + half - 1, seq - 1) - jnp.maximum(t - half, 0) + 1).astype(F32)
        pooled = (fwd + bwd) / count - hg
        mixed = _bdot(pooled, w_ref[g]) * scale[:, cols]
        o_ref[:, cols] = x[:, cols] + gate[:, cols] * mixed


def _pool_layer(x, gain, mods, w_pool, scale):
    row = pl.BlockSpec((ROW_BLOCK, D), lambda i: (i, 0))
    n_g = len(POOL_WINDOWS)
    return pl.pallas_call(
        _pool_kernel,
        out_shape=jax.ShapeDtypeStruct((N_TOK, D), F32),
        grid=(N_TOK // ROW_BLOCK,),
        in_specs=[row, pl.BlockSpec((1, D), lambda i: (0, 0)), _mods_spec(ROW_BLOCK),
                  pl.BlockSpec((n_g, POOL_GROUP, POOL_GROUP), lambda i: (0, 0, 0)),
                  pl.BlockSpec((1, D), lambda i: (0, 0))],
        out_specs=row,
        compiler_params=_params(("parallel",)),
        name="pool",
    )(x, gain.reshape(1, D), mods, w_pool, scale.reshape(1, D))


ATT_Q_TILE = 256


def _head_rms(x, gain):
    return x * lax.rsqrt(jnp.mean(x * x, axis=-1, keepdims=True) + NORM_EPS) * gain


def _attn_kernel(*refs, latent):
    if latent:
        q_ref, k_ref, v_ref, qg_ref, kg_ref, ck_ref, cv_ref, bias_ref, o_ref = refs
    else:
        q_ref, k_ref, v_ref, qg_ref, kg_ref, o_ref, kn_ref = refs
    outs, kns = [], []
    for hh in range(HEADS_PER_STEP):
        sl = slice(hh * HEAD, (hh + 1) * HEAD)
        q = _head_rms(q_ref[:, sl], qg_ref[...]) * (HEAD ** -0.5)
        k = _head_rms(k_ref[:, sl], kg_ref[...])
        v = v_ref[:, sl]
        s = _bdot_nt(q, k)
        if latent:
            s = s + bias_ref[hh]
            s_ctx = _bdot_nt(q, ck_ref[:, sl])
            m = jnp.maximum(jnp.max(s, axis=-1, keepdims=True), jnp.max(s_ctx, axis=-1, keepdims=True))
            p, p_ctx = jnp.exp(s - m), jnp.exp(s_ctx - m)
            den = jnp.sum(p, axis=-1, keepdims=True) + jnp.sum(p_ctx, axis=-1, keepdims=True)
            o = _bdot(p, v) + _bdot(p_ctx, cv_ref[:, sl])
        else:
            m = jnp.max(s, axis=-1, keepdims=True)
            p = jnp.exp(s - m)
            den = jnp.sum(p, axis=-1, keepdims=True)
            o = _bdot(p, v)
            kns.append(k)
        outs.append(o / den)
    o_ref[...] = jnp.concatenate(outs, axis=1)
    if not latent:
        kn_ref[...] = jnp.concatenate(kns, axis=1)


def _attention_prompt(qkv, q_gain, k_gain):
    pair = HEADS // HEADS_PER_STEP
    tok = lambda j: pl.BlockSpec((None, SEQ, LANES), lambda b, p: (j, b, p))
    gain = pl.BlockSpec((1, HEAD), lambda b, p: (0, 0))
    out = pl.BlockSpec((SEQ, LANES), lambda b, p: (b, p))
    return pl.pallas_call(
        functools.partial(_attn_kernel, latent=False),
        out_shape=[jax.ShapeDtypeStruct((N_PROMPT, D), F32)] * 2,
        grid=(BATCH, pair),
        in_specs=[tok(0), tok(1), tok(2), gain, gain],
        out_specs=[out, out],
        compiler_params=_params(("parallel", "parallel")),
        name="attn_prompt",
    )(qkv, qkv, qkv, q_gain.reshape(1, HEAD), k_gain.reshape(1, HEAD))


def _attention_latent(qkv, q_gain, k_gain, ck, cv, bias):
    pair = HEADS // HEADS_PER_STEP
    nq = DEC_SEQ // ATT_Q_TILE
    row0 = N_PROMPT // DEC_SEQ
    gain = pl.BlockSpec((1, HEAD), lambda p, b, i: (0, 0))
    kv = lambda j: pl.BlockSpec((None, DEC_SEQ, LANES), lambda p, b, i: (j, row0 + b, p))
    ctx = pl.BlockSpec((None, PAST_LEN, LANES), lambda p, b, i: (b, 0, p))
    return pl.pallas_call(
        functools.partial(_attn_kernel, latent=True),
        out_shape=jax.ShapeDtypeStruct((N_LATENT, D), F32),
        grid=(pair, DEC_BATCH, nq),
        in_specs=[pl.BlockSpec((None, ATT_Q_TILE, LANES), lambda p, b, i: (0, (row0 + b) * nq + i, p)),
                  kv(1), kv(2), gain, gain, ctx, ctx,
                  pl.BlockSpec((HEADS_PER_STEP, ATT_Q_TILE, DEC_SEQ), lambda p, b, i: (p, i, 0))],
        out_specs=pl.BlockSpec((ATT_Q_TILE, LANES), lambda p, b, i: (b * nq + i, p)),
        compiler_params=_params(("parallel", "parallel", "parallel")),
        name="attn_latent",
    )(qkv, qkv, qkv, q_gain.reshape(1, HEAD), k_gain.reshape(1, HEAD), ck, cv, bias)


def _neighbourhood_bias(rpb):
    rows = DEC_SEQ // GRID_W
    win_r = min(NA_WIN_ROWS, rows)
    pos = np.arange(DEC_SEQ)
    pr, pc = pos // GRID_W, pos % GRID_W
    r0 = np.clip(pr - win_r // 2, 0, rows - win_r)
    c0 = np.clip(pc - NA_WIN_COLS // 2, 0, GRID_W - NA_WIN_COLS)
    inside = ((pr[None, :] >= r0[:, None]) & (pr[None, :] < r0[:, None] + win_r)
              & (pc[None, :] >= c0[:, None]) & (pc[None, :] < c0[:, None] + NA_WIN_COLS))
    dr = np.clip(pr[None, :] - pr[:, None] + NA_WIN_ROWS - 1, 0, 2 * NA_WIN_ROWS - 2)
    dc = np.clip(pc[None, :] - pc[:, None] + NA_WIN_COLS - 1, 0, 2 * NA_WIN_COLS - 2)
    return jnp.where(inside[None], rpb[:, dr, dc], NEG_BIG)


ROUTER_TILE = 512


def _router_kernel(x_ref, g_ref, m_ref, w_ref, b_ref, h_ref, gate_ref):
    h = _rms_mod(x_ref[...], g_ref[...], m_ref[3], m_ref[4])
    h_ref[...] = h
    h_hi = h.astype(BF16)
    h_lo = (h - h_hi.astype(F32)).astype(BF16)
    w = w_ref[...]
    w_hi = w.astype(BF16)
    w_lo = (w - w_hi.astype(F32)).astype(BF16)
    logits = (jnp.dot(h_hi, w_hi, preferred_element_type=F32) + jnp.dot(h_hi, w_lo, preferred_element_type=F32)
              + jnp.dot(h_lo, w_hi, preferred_element_type=F32) + b_ref[...])
    lane = lax.broadcasted_iota(jnp.int32, logits.shape, 1)
    vals = logits
    top0 = None
    den = 0.0
    gates = jnp.full(logits.shape, -1.0, F32)
    for j in range(TOP_K):
        m = jnp.max(vals, axis=-1, keepdims=True)
        first = jnp.min(jnp.where(vals == m, lane, N_EXPERTS), axis=-1, keepdims=True)
        sel = lane == first
        if j == 0:
            top0 = m
        e = jnp.exp(m - top0)
        den = den + e
        gates = jnp.where(sel, e, gates)
        vals = jnp.where(sel, -jnp.inf, vals)
    gate_ref[...] = jnp.where(gates >= 0.0, gates / den, -1.0)


def _router(x, gain, mods, w_router, b_router):
    row = pl.BlockSpec((ROUTER_TILE, D), lambda i: (i, 0))
    return pl.pallas_call(
        _router_kernel,
        out_shape=[jax.ShapeDtypeStruct((N_TOK, D), F32), jax.ShapeDtypeStruct((N_TOK, N_EXPERTS), F32)],
        grid=(N_TOK // ROUTER_TILE,),
        in_specs=[row, pl.BlockSpec((1, D), lambda i: (0, 0)), _mods_spec(ROUTER_TILE),
                  pl.BlockSpec((D, N_EXPERTS), lambda i: (0, 0)), pl.BlockSpec((1, N_EXPERTS), lambda i: (0, 0))],
        out_specs=[row, pl.BlockSpec((ROUTER_TILE, N_EXPERTS), lambda i: (i, 0))],
        compiler_params=_params(("parallel",)),
        name="router",
    )(x, gain.reshape(1, D), mods, w_router, b_router.reshape(1, N_EXPERTS))


def _row_copy(src_hbm, src_row, dst, dst_row, sem):
    return pltpu.make_async_copy(src_hbm.at[pl.ds(src_row, 1)], dst.at[pl.ds(dst_row, 1)], sem)


def _experts_kernel(te_ref, nt_ref, tok_ref, h_hbm, gate_ref, wgu_ref, bgu_ref, wd_ref, bd_ref, y_ref,
                    xbuf, sem, wgu_bf, wd_bf):
    t = pl.program_id(0)

    @pl.when(t < nt_ref[0])
    def _():
        base = t * MOE_TILE

        def issue(r, carry):
            _row_copy(h_hbm, tok_ref[base + r], xbuf, r, sem).start()
            return carry

        lax.fori_loop(0, MOE_TILE, issue, 0)

        @pl.when((t == 0) | (te_ref[t] != te_ref[jnp.maximum(t - 1, 0)]))
        def _():
            wgu_bf[...] = wgu_ref[...].astype(BF16)
            wd_bf[...] = wd_ref[...].astype(BF16)

        def wait(r, carry):
            _row_copy(h_hbm, 0, xbuf, r, sem).wait()
            return carry

        lax.fori_loop(0, MOE_TILE, wait, 0)

        hid = jnp.dot(xbuf[...].astype(BF16), wgu_bf[...], preferred_element_type=F32) + bgu_ref[...]
        glu = jnp.minimum(hid[:, :D_EXPERT], SWIGLU_LIMIT)
        lin = jnp.clip(hid[:, D_EXPERT:], -SWIGLU_LIMIT, SWIGLU_LIMIT)
        act = glu * _sigmoid(SWIGLU_ALPHA * glu) * (lin + 1.0)
        out = jnp.dot(act.astype(BF16), wd_bf[...], preferred_element_type=F32) + bd_ref[...]
        y_ref[...] = out * gate_ref[...]

    @pl.when(t >= nt_ref[0])
    def _():
        y_ref[...] = jnp.zeros_like(y_ref)


def _experts(h, tile_expert, n_tiles, token_of, gate_of, w_gu, b_gu, w_down, b_down):
    e_of = lambda t, te, nt, tok: te[t]
    grid_spec = pltpu.PrefetchScalarGridSpec(
        num_scalar_prefetch=3,
        grid=(MOE_TILES,),
        in_specs=[pl.BlockSpec(memory_space=pl.ANY),
                  pl.BlockSpec((MOE_TILE, 1), lambda t, te, nt, tok: (t, 0)),
                  pl.BlockSpec((None, D, 2 * D_EXPERT), lambda t, te, nt, tok: (te[t], 0, 0)),
                  pl.BlockSpec((None, 1, 2 * D_EXPERT), lambda t, te, nt, tok: (te[t], 0, 0)),
                  pl.BlockSpec((None, D_EXPERT, D), lambda t, te, nt, tok: (te[t], 0, 0)),
                  pl.BlockSpec((None, 1, D), lambda t, te, nt, tok: (te[t], 0, 0))],
        out_specs=pl.BlockSpec((MOE_TILE, D), lambda t, te, nt, tok: (t, 0)),
        scratch_shapes=[pltpu.VMEM((MOE_TILE, D), F32), pltpu.SemaphoreType.DMA(()),
                        pltpu.VMEM((D, 2 * D_EXPERT), BF16), pltpu.VMEM((D_EXPERT, D), BF16)],
    )
    return pl.pallas_call(
        _experts_kernel,
        out_shape=jax.ShapeDtypeStruct((MOE_ROWS, D), F32),
        grid_spec=grid_spec,
        compiler_params=_params(("arbitrary",)),
        name="experts",
    )(tile_expert, n_tiles, token_of, h, gate_of.reshape(MOE_ROWS, 1), w_gu,
      b_gu.reshape(N_EXPERTS, 1, 2 * D_EXPERT), w_down, b_down.reshape(N_EXPERTS, 1, D))


def _combine_kernel(pos_ref, y_hbm, x_ref, m_ref, o_ref, buf, sem):
    base = pl.program_id(0) * COMBINE_TILE * TOP_K

    def issue(r, carry):
        for j in range(TOP_K):
            _row_copy(y_hbm, pos_ref[base + r * TOP_K + j], buf.at[j], r, sem).start()
        return carry

    lax.fori_loop(0, COMBINE_TILE, issue, 0)

    def wait(r, carry):
        for j in range(TOP_K):
            _row_copy(y_hbm, 0, buf.at[j], r, sem).wait()
        return carry

    lax.fori_loop(0, COMBINE_TILE, wait, 0)
    moe = (buf[0] + buf[1]) + (buf[2] + buf[3])
    o_ref[...] = x_ref[...] + m_ref[5] * moe


def _combine(y, pos4, x, mods):
    row = pl.BlockSpec((COMBINE_TILE, D), lambda i, pos: (i, 0))
    per = ROW_BLOCK // COMBINE_TILE
    grid_spec = pltpu.PrefetchScalarGridSpec(
        num_scalar_prefetch=1,
        grid=(N_TOK // COMBINE_TILE,),
        in_specs=[pl.BlockSpec(memory_space=pl.ANY), row,
                  pl.BlockSpec((None, N_MOD, 1, D), lambda i, pos: (_cond_row(i // per), 0, 0, 0))],
        out_specs=row,
        scratch_shapes=[pltpu.VMEM((TOP_K, COMBINE_TILE, D), F32), pltpu.SemaphoreType.DMA(())],
    )
    return pl.pallas_call(
        _combine_kernel,
        out_shape=jax.ShapeDtypeStruct((N_TOK, D), F32),
        grid_spec=grid_spec,
        compiler_params=_params(("arbitrary",)),
        name="combine",
    )(pos4, y, x, mods)


def _moe_layer(x, gain, mods, w_router, b_router, w_gu, b_gu, w_down, b_down):
    h, gates = _router(x, gain, mods, w_router, b_router)
    sel = gates >= 0.0
    sel_i = sel.astype(jnp.int32)
    rank = jnp.cumsum(sel_i, axis=0) - sel_i
    count = jnp.sum(sel_i, axis=0)
    padded = ((count + MOE_TILE - 1) // MOE_TILE) * MOE_TILE
    group_end = jnp.cumsum(padded)
    group_start = group_end - padded
    pos = jnp.where(sel, group_start[None, :] + rank, MOE_ROWS)
    tok_ids = jnp.broadcast_to(jnp.arange(N_TOK, dtype=jnp.int32)[:, None], pos.shape)
    token_of = jnp.zeros((MOE_ROWS,), jnp.int32).at[pos.reshape(-1)].set(tok_ids.reshape(-1), mode="drop")
    gate_of = jnp.zeros((MOE_ROWS,), F32).at[pos.reshape(-1)].set(gates.reshape(-1), mode="drop")
    pos4 = jnp.sort(pos, axis=1)[:, :TOP_K].reshape(-1).astype(jnp.int32)
    n_tiles = (group_end[-1] // MOE_TILE).astype(jnp.int32)
    tile_start = jnp.arange(MOE_TILES, dtype=jnp.int32) * MOE_TILE
    tile_expert = jnp.searchsorted(group_end, jnp.minimum(tile_start, group_end[-1] - 1), side="right")
    tile_expert = jnp.minimum(tile_expert, N_EXPERTS - 1).astype(jnp.int32)
    y = _experts(h, tile_expert, n_tiles.reshape(1), token_of, gate_of, w_gu, b_gu, w_down, b_down)
    return _combine(y, pos4, x, mods)


def _rwkv_layer(x, gain, mods, state, p):
    (mu, w_rkv, w_out, w0, w1, w2, a0, a1, a2, g1, g2, k_k, k_a, r_k, lnx_w, lnx_b) = p
    h, xx = _normmod(x, gain, mods, with_xx=True)
    mu = mu.reshape(N_MOD, 1, D)
    rkv = _proj3(h, w_rkv, xx, mu[jnp.array([0, 2, 3])])
    wl = _lora(h, xx, mu[jnp.array([1, 1])], w1, w2, w0.reshape(2, 1, D), "tanh")
    al = _lora(h, xx, mu[jnp.array([4, 4])], a1, a2, a0.reshape(2, 1, D), "none")
    g = _lora(h, xx, mu[5:6], g1[None], g2[None], None, "sigmoid")[0]
    vecs = (k_k, k_a, r_k, lnx_w, lnx_b)
    z_p, s_ctx = _wkv(rkv, wl, al, None, *vecs, batch=BATCH, seq=SEQ, row_block0=0, emit_state=True)
    (z_s,) = _wkv(rkv, wl, al, state, *vecs, batch=DEC_BATCH, seq=DEC_SEQ,
                  row_block0=N_PROMPT // DEC_SEQ, emit_state=False)
    z = jnp.concatenate([z_p, z_s], axis=0)
    return _mmres(z, w_out, x, mods, g=g), s_ctx


def _na_layer(x, gain, mods, ck, cv, p):
    w_qkv, w_out, q_norm, k_norm, rpb = p
    (h,) = _normmod(x, gain, mods, with_xx=False)
    qkv = _proj3(h, w_qkv)
    o_p, k_p = _attention_prompt(qkv, q_norm, k_norm)
    o_s = _attention_latent(qkv, q_norm, k_norm, ck.reshape(DEC_BATCH, PAST_LEN, D),
                            cv.reshape(DEC_BATCH, PAST_LEN, D), _neighbourhood_bias(rpb))
    o = jnp.concatenate([o_p, o_s], axis=0)
    new_k = k_p.reshape(BATCH, SEQ, HEADS, HEAD)
    new_v = qkv[2, :N_PROMPT].reshape(BATCH, SEQ, HEADS, HEAD)
    return _mmres(o, w_out, x, mods), new_k, new_v


def kernel(x_prompt, x_sample, c, c_ctx, state_wkv, cache_k, cache_v, norm_mix, norm_ffn, w_mod, b_mod, rw_mu, rw_w_rkv, rw_w_out, rw_w0, rw_w1, rw_w2, rw_a0, rw_a1, rw_a2, rw_g1, rw_g2, rw_k_k, rw_k_a, rw_r_k, rw_lnx_w, rw_lnx_b, pool_w, pool_scale, na_w_qkv, na_w_out, na_q_norm, na_k_norm, na_rpb, moe_w_router, moe_b_router, moe_w_gu, moe_b_gu, moe_w_down, moe_b_down):
    x = jnp.concatenate([x_prompt.reshape(N_PROMPT, D), x_sample.reshape(N_LATENT, D)], axis=0)
    cond = jnp.concatenate([c_ctx[None, :], c, jnp.zeros((COND_ROWS - 1 - DEC_BATCH, D), F32)], axis=0)
    mods = _adaln(cond, w_mod, b_mod)
    new_wkv, new_k, new_v = [], [], []
    for i in range(DEPTH):
        kind, slot = i % N_MIXERS, i // N_MIXERS
        if kind == 0:
            rw = (rw_mu[slot], rw_w_rkv[slot], rw_w_out[slot], rw_w0[slot], rw_w1[slot], rw_w2[slot],
                  rw_a0[slot], rw_a1[slot], rw_a2[slot], rw_g1[slot], rw_g2[slot], rw_k_k[slot],
                  rw_k_a[slot], rw_r_k[slot], rw_lnx_w[slot], rw_lnx_b[slot])
            x, s_ctx = _rwkv_layer(x, norm_mix[i], mods[i], state_wkv[:, slot], rw)
            new_wkv.append(s_ctx)
        elif kind == 1:
            x = _pool_layer(x, norm_mix[i], mods[i], pool_w[slot], pool_scale[slot])
        else:
            na = (na_w_qkv[slot], na_w_out[slot], na_q_norm[slot], na_k_norm[slot], na_rpb[slot])
            x, k_p, v_p = _na_layer(x, norm_mix[i], mods[i], cache_k[:, slot], cache_v[:, slot], na)
            new_k.append(k_p)
            new_v.append(v_p)
        x = _moe_layer(x, norm_ffn[i], mods[i], moe_w_router[i], moe_b_router[i], moe_w_gu[i], moe_b_gu[i],
                       moe_w_down[i], moe_b_down[i])
    y_prompt = x[:N_PROMPT].reshape(BATCH, SEQ, D)
    y_sample = x[N_PROMPT:].reshape(DEC_BATCH, DEC_SEQ, D)
    return (y_prompt, y_sample, jnp.stack(new_wkv, axis=1), jnp.stack(new_k, axis=1), jnp.stack(new_v, axis=1))
```

```python
import functools

import jax
import jax.numpy as jnp
import numpy as np
from jax import lax
from jax.experimental import pallas as pl
from jax.experimental.pallas import tpu as pltpu

F32 = jnp.float32
BF16 = jnp.bfloat16

D = 1024
BATCH, SEQ = 16, 256
DEC_BATCH, DEC_SEQ = 4, 1024
DEPTH = 4
PAST_LEN = 512
GRID_W = 64
N_MIXERS = 3
N_MOD = 6
NORM_EPS = 1e-6
HEAD = 64
HEADS = D // HEAD
RW_GN_EPS = 64e-5
POOL_WINDOWS = (2, 4, 8, 16)
POOL_GROUP = D // len(POOL_WINDOWS)
NA_WIN_ROWS, NA_WIN_COLS = 8, 16
N_EXPERTS, TOP_K = 32, 4
D_EXPERT = D
SWIGLU_LIMIT, SWIGLU_ALPHA = 7.0, 1.702

N_PROMPT = BATCH * SEQ
N_LATENT = DEC_BATCH * DEC_SEQ
N_TOK = N_PROMPT + N_LATENT
COND_ROWS = 8
ROW_BLOCK = 1024
PROMPT_BLOCKS = N_PROMPT // ROW_BLOCK
LANES = 128
HEADS_PER_STEP = LANES // HEAD
CHUNK = 64
WKV_GROUP = 4
MOE_TILE = 256
MOE_ROWS = N_TOK * TOP_K + N_EXPERTS * MOE_TILE
MOE_TILES = MOE_ROWS // MOE_TILE
COMBINE_TILE = 128
NEG_BIG = -1e30
VMEM_LIMIT = 56 * 1024 * 1024


def _cond_row(block_1024):
    return jnp.maximum(block_1024 - (PROMPT_BLOCKS - 1), 0)


def _mods_spec(rows_per_block):
    per = ROW_BLOCK // rows_per_block
    return pl.BlockSpec((None, N_MOD, 1, D), lambda *ids: (_cond_row(ids[-1] // per), 0, 0, 0))


def _params(sem):
    return pltpu.CompilerParams(dimension_semantics=sem, vmem_limit_bytes=VMEM_LIMIT)


def _bdot(a, b):
    return jnp.dot(a.astype(BF16), b.astype(BF16), preferred_element_type=F32)


def _bdot_nt(a, b):
    return lax.dot_general(a.astype(BF16), b.astype(BF16), (((1,), (1,)), ((), ())),
                           preferred_element_type=F32)


def _bdot_tn(a, b):
    return lax.dot_general(a.astype(BF16), b.astype(BF16), (((0,), (0,)), ((), ())),
                           preferred_element_type=F32)


def _split3(x):
    hi = x.astype(BF16)
    r1 = x - hi.astype(F32)
    mid = r1.astype(BF16)
    lo = (r1 - mid.astype(F32)).astype(BF16)
    return hi, mid, lo


def _sigmoid(x):
    return 1.0 / (1.0 + jnp.exp(-x))


def _rms_mod(x, gain, shift, scale):
    y = x * lax.rsqrt(jnp.mean(x * x, axis=-1, keepdims=True) + NORM_EPS)
    return (y * gain) * (1.0 + scale) + shift


def _adaln_kernel(c_ref, w_ref, b_ref, o_ref):
    c = c_ref[...]
    s = c * _sigmoid(c)
    s_hi = s.astype(BF16)
    s_lo = (s - s_hi.astype(F32)).astype(BF16)
    w = w_ref[...]
    w_hi = w.astype(BF16)
    w_lo = (w - w_hi.astype(F32)).astype(BF16)
    o_ref[...] = (jnp.dot(s_hi, w_hi, preferred_element_type=F32) + jnp.dot(s_lo, w_hi, preferred_element_type=F32)
                  + jnp.dot(s_hi, w_lo, preferred_element_type=F32) + b_ref[...])


def _adaln(cond, w_mod, b_mod):
    out = pl.pallas_call(
        _adaln_kernel,
        out_shape=jax.ShapeDtypeStruct((DEPTH, COND_ROWS, N_MOD * D), F32),
        grid=(DEPTH, N_MOD),
        in_specs=[pl.BlockSpec((COND_ROWS, D), lambda l, j: (0, 0)),
                  pl.BlockSpec((None, D, D), lambda l, j: (l, 0, j)),
                  pl.BlockSpec((None, 1, D), lambda l, j: (l, 0, j))],
        out_specs=pl.BlockSpec((None, COND_ROWS, D), lambda l, j: (l, 0, j)),
        compiler_params=_params(("parallel", "parallel")),
        name="adaln",
    )(cond, w_mod, b_mod.reshape(DEPTH, 1, N_MOD * D))
    return out.reshape(DEPTH, COND_ROWS, N_MOD, 1, D)


def _normmod_kernel(x_ref, g_ref, m_ref, h_ref, *xx_ref, shift_idx, scale_idx):
    h = _rms_mod(x_ref[...], g_ref[...], m_ref[shift_idx], m_ref[scale_idx])
    h_ref[...] = h
    if xx_ref:
        seq = jnp.where(pl.program_id(0) < PROMPT_BLOCKS, SEQ, DEC_SEQ)
        t = lax.broadcasted_iota(jnp.int32, (ROW_BLOCK, 1), 0) & (seq - 1)
        prev = jnp.where(t == 0, 0.0, pltpu.roll(h, 1, 0))
        nxt = jnp.where(t == seq - 1, 0.0, pltpu.roll(h, ROW_BLOCK - 1, 0))
        xx_ref[0][...] = 0.5 * (prev + nxt) - h


def _normmod(x, gain, mods, with_xx):
    n_out = 2 if with_xx else 1
    row = pl.BlockSpec((ROW_BLOCK, D), lambda i: (i, 0))
    outs = pl.pallas_call(
        functools.partial(_normmod_kernel, shift_idx=0, scale_idx=1),
        out_shape=[jax.ShapeDtypeStruct((N_TOK, D), F32)] * n_out,
        grid=(N_TOK // ROW_BLOCK,),
        in_specs=[row, pl.BlockSpec((1, D), lambda i: (0, 0)), _mods_spec(ROW_BLOCK)],
        out_specs=[row] * n_out,
        compiler_params=_params(("parallel",)),
        name="normmod",
    )(x, gain.reshape(1, D), mods)
    return outs


PROJ_TILE = 512


def _proj3_kernel(*refs, mix):
    if mix:
        h_ref, xx_ref, mu_ref, w_ref, o_ref, wbf_ref = refs
    else:
        h_ref, w_ref, o_ref, wbf_ref = refs

    @pl.when(pl.program_id(1) == 0)
    def _():
        wbf_ref[...] = w_ref[...].astype(BF16)

    x = h_ref[...]
    if mix:
        x = x + xx_ref[...] * mu_ref[...]
    o_ref[...] = jnp.dot(x.astype(BF16), wbf_ref[...], preferred_element_type=F32)


def _proj3(h, w, xx=None, mu=None):
    mix = xx is not None
    row = pl.BlockSpec((PROJ_TILE, D), lambda j, i: (i, 0))
    if mix:
        ins = [h, xx, mu, w]
        specs = [row, row, pl.BlockSpec((None, 1, D), lambda j, i: (j, 0, 0)),
                 pl.BlockSpec((None, D, D), lambda j, i: (j, 0, 0))]
    else:
        ins = [h, w]
        specs = [row, pl.BlockSpec((D, D), lambda j, i: (0, j))]
    return pl.pallas_call(
        functools.partial(_proj3_kernel, mix=mix),
        out_shape=jax.ShapeDtypeStruct((3, N_TOK, D), F32),
        grid=(3, N_TOK // PROJ_TILE),
        in_specs=specs,
        out_specs=pl.BlockSpec((None, PROJ_TILE, D), lambda j, i: (j, i, 0)),
        scratch_shapes=[pltpu.VMEM((D, D), BF16)],
        compiler_params=_params(("arbitrary", "arbitrary")),
        name="proj3",
    )(*ins)


def _lora_kernel(h_ref, xx_ref, mu_ref, a_ref, b_ref, *rest, n, act, has_bias):
    if has_bias:
        bias_ref, o_ref = rest
    else:
        (o_ref,) = rest
    h = h_ref[...]
    xx = xx_ref[...]
    for j in range(n):
        x = h + xx * mu_ref[j]
        t = _bdot(x, a_ref[j])
        if act == "tanh":
            t = jnp.tanh(t)
        elif act == "sigmoid":
            t = _sigmoid(t)
        o = _bdot(t, b_ref[j])
        if has_bias:
            o = o + bias_ref[j]
        o_ref[j] = o


def _lora(h, xx, mu, a, b, bias, act):
    n, _, r = a.shape
    row = pl.BlockSpec((PROJ_TILE, D), lambda i: (i, 0))
    full = lambda shape: pl.BlockSpec(shape, lambda i: (0,) * len(shape))
    ins = [h, xx, mu, a, b]
    specs = [row, row, full((n, 1, D)), full((n, D, r)), full((n, r, D))]
    if bias is not None:
        ins.append(bias)
        specs.append(full((n, 1, D)))
    return pl.pallas_call(
        functools.partial(_lora_kernel, n=n, act=act, has_bias=bias is not None),
        out_shape=jax.ShapeDtypeStruct((n, N_TOK, D), F32),
        grid=(N_TOK // PROJ_TILE,),
        in_specs=specs,
        out_specs=pl.BlockSpec((n, PROJ_TILE, D), lambda i: (0, i, 0)),
        compiler_params=_params(("parallel",)),
        name="lora_" + act,
    )(*ins)


def _mmres_kernel(*refs, with_g, gate_idx):
    if with_g:
        z_ref, g_ref, w_ref, x_ref, m_ref, o_ref, wbf_ref = refs
    else:
        z_ref, w_ref, x_ref, m_ref, o_ref, wbf_ref = refs

    @pl.when(pl.program_id(0) == 0)
    def _():
        wbf_ref[...] = w_ref[...].astype(BF16)

    z = z_ref[...]
    if with_g:
        z = z * g_ref[...]
    o_ref[...] = x_ref[...] + m_ref[gate_idx] * jnp.dot(z.astype(BF16), wbf_ref[...],
                                                        preferred_element_type=F32)


def _mmres(z, w, x, mods, g=None):
    row = pl.BlockSpec((PROJ_TILE, D), lambda i: (i, 0))
    ins, specs = [z], [row]
    if g is not None:
        ins.append(g)
        specs.append(row)
    ins += [w, x, mods]
    specs += [pl.BlockSpec((D, D), lambda i: (0, 0)), row, _mods_spec(PROJ_TILE)]
    return pl.pallas_call(
        functools.partial(_mmres_kernel, with_g=g is not None, gate_idx=2),
        out_shape=jax.ShapeDtypeStruct((N_TOK, D), F32),
        grid=(N_TOK // PROJ_TILE,),
        in_specs=specs,
        out_specs=row,
        scratch_shapes=[pltpu.VMEM((D, D), BF16)],
        compiler_params=_params(("arbitrary",)),
        name="mmres",
    )(*ins)


def _wkv_kernel(*refs, seq, has_s0, emit_state):
    it = iter(refs)
    rkv_ref, wl_ref, al_ref = next(it), next(it), next(it)
    s0_ref = next(it) if has_s0 else None
    kk_ref, ka_ref, rk_ref, lw_ref, lb_ref = next(it), next(it), next(it), next(it), next(it)
    z_ref = next(it)
    sf_ref = next(it) if emit_state else None
    w2_ref, of_ref, ry_ref, pc_ref, y_ref = next(it), next(it), next(it), next(it), next(it)
    n_chunks = seq // CHUNK

    ri = lax.broadcasted_iota(jnp.int32, (CHUNK, CHUNK), 0)
    ci = lax.broadcasted_iota(jnp.int32, (CHUNK, CHUNK), 1)
    eye = (ri == ci).astype(F32)
    blk16 = (ri // 16) == (ci // 16)
    blk32 = (ri // 32) == (ci // 32)
    incl = (ri >= ci, ri <= ci)
    strict = (ri > ci, ri < ci)
    k_k, k_a, r_k = kk_ref[...], ka_ref[...], rk_ref[...]

    tri_bf = tuple(m.astype(BF16) for m in incl)
    mask2 = tuple(jnp.concatenate([strict[d], incl[d]], axis=0) for d in range(2))
    heads = tuple(slice(hh * HEAD, (hh + 1) * HEAD) for hh in range(HEADS_PER_STEP))

    def prepare(g, carry):
        pairs = []
        for j in range(WKV_GROUP):
            cc = g * WKV_GROUP + j
            rows = pl.ds(pl.multiple_of(cc * CHUNK, CHUNK), CHUNK)
            r2, k2, v2 = rkv_ref[0, rows, :], rkv_ref[1, rows, :], rkv_ref[2, rows, :]
            kk2 = k2 * k_k
            kk2 = jnp.concatenate(
                [kk2[:, sl] / jnp.maximum(jnp.sqrt(jnp.sum(kk2[:, sl] * kk2[:, sl], axis=-1, keepdims=True)), 1e-12)
                 for sl in heads], axis=1)
            for d in range(2):
                w_in = -wl_ref[d, rows, :]
                softplus = jnp.maximum(w_in, 0.0) + jnp.log(1.0 + jnp.exp(-jnp.abs(w_in)))
                logdec = -jnp.exp(-softplus - 0.5)
                a2 = _sigmoid(al_ref[d, rows, :])
                hi, mid, lo = _split3(logdec)
                lp = (jnp.dot(tri_bf[d], hi, preferred_element_type=F32)
                      + jnp.dot(tri_bf[d], mid, preferred_element_type=F32)
                      + jnp.dot(tri_bf[d], lo, preferred_element_type=F32))
                lp_end = lp[CHUNK - 1:CHUNK, :] if d == 0 else lp[0:1, :]
                e_neg, p_end = jnp.exp(-lp), jnp.exp(lp_end)
                kt2 = kk2 * jnp.exp(lp - logdec)
                rt2 = r2 * jnp.exp(lp)
                ks2 = k2 * (1.0 + (a2 - 1.0) * k_a) * e_neg
                bs2 = kk2 * a2 * e_neg
                pc_ref[d, cc] = p_end
                pairs.append((cc, rows, d, kt2, rt2, ks2, bs2, ks2 * p_end, bs2 * p_end, v2))
        chains = [(pi, hh, sl) for pi in range(len(pairs)) for hh, sl in enumerate(heads)]
        a_k, a_b = [], []
        for pi, hh, sl in chains:
            _, _, d, kt2, rt2, ks2, bs2, _, _, _ = pairs[pi]
            q2 = jnp.concatenate([kt2[:, sl], rt2[:, sl]], axis=0)
            a_k.append(jnp.where(mask2[d], _bdot_nt(q2, ks2[:, sl]), 0.0))
            a_b.append(jnp.where(mask2[d], _bdot_nt(q2, bs2[:, sl]), 0.0))
        av = [_bdot(a_k[n], pairs[pi][9][:, sl]) for n, (pi, hh, sl) in enumerate(chains)]
        d16 = [jnp.where(blk16, a[:CHUNK], 0.0) for a in a_b]
        x = [eye - t for t in d16]
        p = [_bdot(t, t) for t in d16]
        for level in range(3):
            x = [xn + _bdot(xn, pn) for xn, pn in zip(x, p)]
            if level < 2:
                p = [_bdot(pn, pn) for pn in p]
        for inner, outer in ((blk16, blk32), (blk32, None)):
            keep = (~inner) if outer is None else (outer & (~inner))
            t = [_bdot(xn, jnp.where(keep, a[:CHUNK], 0.0)) for xn, a in zip(x, a_b)]
            x = [xn - _bdot(tn, xn) for xn, tn in zip(x, t)]
        wu = [_bdot(x[n], jnp.concatenate([pairs[pi][3][:, sl], av[n][:CHUNK]], axis=1))
              for n, (pi, hh, sl) in enumerate(chains)]
        corr = [_bdot(a_b[n][CHUNK:], wu[n]) for n in range(len(chains))]
        yos = {}
        for n, (pi, hh, sl) in enumerate(chains):
            cc, rows, d, kt2, rt2, ks2, bs2, kh2, bh2, v2 = pairs[pi]
            ry_ref[d, hh, cc] = (rt2[:, sl] - corr[n][:, :HEAD]).astype(BF16)
            yos.setdefault(pi, []).append(av[n][CHUNK:] - corr[n][:, HEAD:])
            w2_ref[d, hh, cc] = _bdot_tn(wu[n][:, :HEAD], bh2[:, sl]).astype(BF16)
            of_ref[d, hh, cc] = _bdot_tn(jnp.concatenate([v2[:, sl], -wu[n][:, HEAD:]], axis=0),
                                         jnp.concatenate([kh2[:, sl], bh2[:, sl]], axis=0))
        for pi, ys in yos.items():
            y_ref[pairs[pi][2], pairs[pi][1], :] = jnp.concatenate(ys, axis=1)
        return carry

    lax.fori_loop(0, n_chunks // WKV_GROUP, prepare, 0)

    def advance(c, states):
        new = []
        for d in range(2):
            cc = c if d == 0 else n_chunks - 1 - c
            rows = pl.ds(pl.multiple_of(cc * CHUNK, CHUNK), CHUNK)
            p_end = pc_ref[d, cc]
            ys = []
            for hh, sl in enumerate(heads):
                s = states[d * HEADS_PER_STEP + hh]
                ys.append(_bdot_nt(ry_ref[d, hh, cc], s))
                new.append(s * p_end[:, sl] - _bdot(s, w2_ref[d, hh, cc]) + of_ref[d, hh, cc])
            y_ref[d, rows, :] = y_ref[d, rows, :] + jnp.concatenate(ys, axis=1)
        return tuple(new)

    if has_s0:
        init = tuple(s0_ref[d, hh] for d in range(2) for hh in range(HEADS_PER_STEP))
    else:
        init = tuple(jnp.zeros((HEAD, HEAD), F32) for _ in range(2 * HEADS_PER_STEP))
    final = lax.fori_loop(0, n_chunks, advance, init)

    if emit_state:
        for d in range(2):
            for hh in range(HEADS_PER_STEP):
                sf_ref[d, hh] = final[d * HEADS_PER_STEP + hh]

    EP = 256
    lnx_w, lnx_b = lw_ref[...], lb_ref[...]

    def epilogue(i, carry):
        rows = pl.ds(pl.multiple_of(i * EP, EP), EP)
        r2, k2, v2 = rkv_ref[0, rows, :], rkv_ref[1, rows, :], rkv_ref[2, rows, :]
        y2 = y_ref[0, rows, :] + y_ref[1, rows, :]
        coef = 2.0 + (_sigmoid(al_ref[0, rows, :]) + _sigmoid(al_ref[1, rows, :]) - 2.0) * k_a
        rkr = r2 * k2 * coef * r_k
        outs = []
        for hh in range(HEADS_PER_STEP):
            sl = slice(hh * HEAD, (hh + 1) * HEAD)
            y = y2[:, sl]
            mean = jnp.mean(y, axis=-1, keepdims=True)
            var = jnp.mean(jnp.square(y - mean), axis=-1, keepdims=True)
            yn = (y - mean) * lax.rsqrt(var + RW_GN_EPS)
            bonus = jnp.sum(rkr[:, sl], axis=-1, keepdims=True) * v2[:, sl]
            outs.append((yn, bonus))
        yn = jnp.concatenate([o[0] for o in outs], axis=1)
        bonus = jnp.concatenate([o[1] for o in outs], axis=1)
        z_ref[rows, :] = yn * lnx_w + lnx_b + bonus
        return carry

    lax.fori_loop(0, seq // EP, epilogue, 0)


def _wkv(rkv, wl, al, s0, k_k, k_a, r_k, lnx_w, lnx_b, *, batch, seq, row_block0, emit_state):
    has_s0 = s0 is not None
    pair = HEADS // HEADS_PER_STEP
    n_chunks = seq // CHUNK
    tok = lambda lead: pl.BlockSpec((lead, seq, LANES), lambda b, p: (0, row_block0 + b, p))
    vec = pl.BlockSpec((1, LANES), lambda b, p: (0, p))
    st = pl.BlockSpec((None, 2, HEADS_PER_STEP, HEAD, HEAD), lambda b, p: (b, 0, p, 0, 0))
    ins, specs = [rkv, wl, al], [tok(3), tok(2), tok(2)]
    if has_s0:
        ins.append(s0)
        specs.append(st)
    ins += [k_k.reshape(1, D), k_a.reshape(1, D), r_k.reshape(1, D), lnx_w.reshape(1, D), lnx_b.reshape(1, D)]
    specs += [vec] * 5
    out_shape = [jax.ShapeDtypeStruct((batch * seq, D), F32)]
    out_specs = [pl.BlockSpec((seq, LANES), lambda b, p: (b, p))]
    if emit_state:
        out_shape.append(jax.ShapeDtypeStruct((batch, 2, HEADS, HEAD, HEAD), F32))
        out_specs.append(st)
    return pl.pallas_call(
        functools.partial(_wkv_kernel, seq=seq, has_s0=has_s0, emit_state=emit_state),
        out_shape=out_shape,
        grid=(batch, pair),
        in_specs=specs,
        out_specs=out_specs,
        scratch_shapes=[pltpu.VMEM((2, HEADS_PER_STEP, n_chunks, HEAD, HEAD), BF16),
                        pltpu.VMEM((2, HEADS_PER_STEP, n_chunks, HEAD, HEAD), F32),
                        pltpu.VMEM((2, HEADS_PER_STEP, n_chunks, CHUNK, HEAD), BF16),
                        pltpu.VMEM((2, n_chunks, 1, LANES), F32),
                        pltpu.VMEM((2, seq, LANES), F32)],
        compiler_params=_params(("parallel", "parallel")),
        name="wkv_%d" % seq,
    )(*ins)


def _pool_kernel(x_ref, g_ref, m_ref, w_ref, sc_ref, o_ref):
    x = x_ref[...]
    h = _rms_mod(x, g_ref[...], m_ref[0], m_ref[1])
    seq = jnp.where(pl.program_id(0) < PROMPT_BLOCKS, SEQ, DEC_SEQ)
    t = lax.broadcasted_iota(jnp.int32, (ROW_BLOCK, 1), 0) & (seq - 1)
    gate = m_ref[2]
    scale = sc_ref[...]
    for g, win in enumerate(POOL_WINDOWS):
        half = win // 2
        cols = slice(g * POOL_GROUP, (g + 1) * POOL_GROUP)
        hg = h[:, cols]
        up = lambda z, m: jnp.where(t + m <= seq - 1, pltpu.roll(z, ROW_BLOCK - m, 0), 0.0)
        down = lambda z, m: jnp.where(t - m >= 0, pltpu.roll(z, m, 0), 0.0)
        fwd = hg
        bwd = down(hg, 1)
        m = 1
        while m < half:
            fwd = fwd + up(fwd, m)
            bwd = bwd + down(bwd, m)
            m *= 2
        count = (jnp.minimum(t + half - 1, seq - 1) - jnp.maximum(t - half, 0) + 1).astype(F32)
        pooled = (fwd + bwd) / count - hg
        mixed = _bdot(pooled, w_ref[g]) * scale[:, cols]
        o_ref[:, cols] = x[:, cols] + gate[:, cols] * mixed


def _pool_layer(x, gain, mods, w_pool, scale):
    row = pl.BlockSpec((ROW_BLOCK, D), lambda i: (i, 0))
    n_g = len(POOL_WINDOWS)
    return pl.pallas_call(
        _pool_kernel,
        out_shape=jax.ShapeDtypeStruct((N_TOK, D), F32),
        grid=(N_TOK // ROW_BLOCK,),
        in_specs=[row, pl.BlockSpec((1, D), lambda i: (0, 0)), _mods_spec(ROW_BLOCK),
                  pl.BlockSpec((n_g, POOL_GROUP, POOL_GROUP), lambda i: (0, 0, 0)),
                  pl.BlockSpec((1, D), lambda i: (0, 0))],
        out_specs=row,
        compiler_params=_params(("parallel",)),
        name="pool",
    )(x, gain.reshape(1, D), mods, w_pool, scale.reshape(1, D))


NA_ROWS = DEC_SEQ // GRID_W
NA_WIN_R = min(NA_WIN_ROWS, NA_ROWS)


def _head_rms(x, gain):
    return x * lax.rsqrt(jnp.mean(x * x, axis=-1, keepdims=True) + NORM_EPS) * gain


def _attn_prompt_kernel(q_ref, k_ref, v_ref, qg_ref, kg_ref, o_ref, kn_ref):
    outs, kns = [], []
    for hh in range(HEADS_PER_STEP):
        sl = slice(hh * HEAD, (hh + 1) * HEAD)
        q = _head_rms(q_ref[:, sl], qg_ref[...]) * (HEAD ** -0.5)
        k = _head_rms(k_ref[:, sl], kg_ref[...])
        s = _bdot_nt(q, k)
        p = jnp.exp(s - jnp.max(s, axis=-1, keepdims=True))
        outs.append(_bdot(p, v_ref[:, sl]) / jnp.sum(p, axis=-1, keepdims=True))
        kns.append(k)
    o_ref[...] = jnp.concatenate(outs, axis=1)
    kn_ref[...] = jnp.concatenate(kns, axis=1)


def _attn_latent_kernel(q_ref, k_ref, v_ref, qg_ref, kg_ref, ck_ref, cv_ref, bias_ref, o_ref):
    outs = []
    for hh in range(HEADS_PER_STEP):
        sl = slice(hh * HEAD, (hh + 1) * HEAD)
        q = _head_rms(q_ref[:, sl], qg_ref[...]) * (HEAD ** -0.5)
        k = _head_rms(k_ref[:, sl], kg_ref[...])
        v = v_ref[:, sl]
        s_ctx = _bdot_nt(q, ck_ref[:, sl])
        m_ctx = jnp.max(s_ctx, axis=-1, keepdims=True)
        p_ctx = jnp.exp(s_ctx - m_ctx)
        l_ctx = jnp.sum(p_ctx, axis=-1, keepdims=True)
        o_ctx = _bdot(p_ctx, cv_ref[:, sl])
        rows_out = []
        for qr in range(NA_ROWS):
            r0 = min(max(qr - NA_WIN_R // 2, 0), NA_ROWS - NA_WIN_R)
            qs = slice(qr * GRID_W, (qr + 1) * GRID_W)
            ks = slice(r0 * GRID_W, (r0 + NA_WIN_R) * GRID_W)
            bias = jnp.concatenate([bias_ref[hh, r0 + j - qr + NA_WIN_ROWS - 1] for j in range(NA_WIN_R)], axis=1)
            s = _bdot_nt(q[qs], k[ks]) + bias
            m = jnp.maximum(jnp.max(s, axis=-1, keepdims=True), m_ctx[qs])
            p = jnp.exp(s - m)
            w_ctx = jnp.exp(m_ctx[qs] - m)
            den = jnp.sum(p, axis=-1, keepdims=True) + l_ctx[qs] * w_ctx
            rows_out.append((_bdot(p, v[ks]) + o_ctx[qs] * w_ctx) / den)
        outs.append(jnp.concatenate(rows_out, axis=0))
    o_ref[...] = jnp.concatenate(outs, axis=1)


def _attention_prompt(qkv, q_gain, k_gain):
    pair = HEADS // HEADS_PER_STEP
    tok = lambda j: pl.BlockSpec((None, SEQ, LANES), lambda b, p: (j, b, p))
    gain = pl.BlockSpec((1, HEAD), lambda b, p: (0, 0))
    out = pl.BlockSpec((SEQ, LANES), lambda b, p: (b, p))
    return pl.pallas_call(
        _attn_prompt_kernel,
        out_shape=[jax.ShapeDtypeStruct((N_PROMPT, D), F32)] * 2,
        grid=(BATCH, pair),
        in_specs=[tok(0), tok(1), tok(2), gain, gain],
        out_specs=[out, out],
        compiler_params=_params(("parallel", "parallel")),
        name="attn_prompt",
    )(qkv, qkv, qkv, q_gain.reshape(1, HEAD), k_gain.reshape(1, HEAD))


def _attention_latent(qkv, q_gain, k_gain, ck, cv, bias):
    pair = HEADS // HEADS_PER_STEP
    row0 = N_PROMPT // DEC_SEQ
    gain = pl.BlockSpec((1, HEAD), lambda p, b: (0, 0))
    tok = lambda j: pl.BlockSpec((None, DEC_SEQ, LANES), lambda p, b: (j, row0 + b, p))
    ctx = pl.BlockSpec((None, PAST_LEN, LANES), lambda p, b: (b, 0, p))
    n_dr = 2 * NA_WIN_ROWS - 1
    return pl.pallas_call(
        _attn_latent_kernel,
        out_shape=jax.ShapeDtypeStruct((N_LATENT, D), F32),
        grid=(pair, DEC_BATCH),
        in_specs=[tok(0), tok(1), tok(2), gain, gain, ctx, ctx,
                  pl.BlockSpec((HEADS_PER_STEP, n_dr, GRID_W, GRID_W), lambda p, b: (p, 0, 0, 0))],
        out_specs=pl.BlockSpec((DEC_SEQ, LANES), lambda p, b: (b, p)),
        compiler_params=_params(("parallel", "parallel")),
        name="attn_latent",
    )(qkv, qkv, qkv, q_gain.reshape(1, HEAD), k_gain.reshape(1, HEAD), ck, cv, bias)


def _column_bias_table(rpb):
    n_dr, n_dc = rpb.shape[1], rpb.shape[2]
    span = 2 * GRID_W - 1
    left = GRID_W - NA_WIN_COLS
    g = jnp.pad(rpb, ((0, 0), (0, 0), (left, span - n_dc - left)))
    flat = jnp.broadcast_to(g[:, :, None, :], (HEADS, n_dr, GRID_W, span)).reshape(HEADS, n_dr, GRID_W * span)
    table = flat[:, :, GRID_W - 1:GRID_W - 1 + GRID_W * (span - 1)].reshape(HEADS, n_dr, GRID_W, span - 1)
    table = table[..., :GRID_W]
    col = np.arange(GRID_W)
    c0 = np.clip(col - NA_WIN_COLS // 2, 0, GRID_W - NA_WIN_COLS)
    inside = (col[None, :] >= c0[:, None]) & (col[None, :] < c0[:, None] + NA_WIN_COLS)
    return jnp.where(inside[None, None], table, NEG_BIG)


ROUTER_TILE = 512


def _router_kernel(x_ref, g_ref, m_ref, w_ref, b_ref, h_ref, gate_ref):
    h = _rms_mod(x_ref[...], g_ref[...], m_ref[3], m_ref[4])
    h_ref[...] = h
    h_hi = h.astype(BF16)
    h_lo = (h - h_hi.astype(F32)).astype(BF16)
    w = w_ref[...]
    w_hi = w.astype(BF16)
    w_lo = (w - w_hi.astype(F32)).astype(BF16)
    logits = (jnp.dot(h_hi, w_hi, preferred_element_type=F32) + jnp.dot(h_hi, w_lo, preferred_element_type=F32)
              + jnp.dot(h_lo, w_hi, preferred_element_type=F32) + b_ref[...])
    lane = lax.broadcasted_iota(jnp.int32, logits.shape, 1)
    vals = logits
    top0 = None
    den = 0.0
    gates = jnp.full(logits.shape, -1.0, F32)
    for j in range(TOP_K):
        m = jnp.max(vals, axis=-1, keepdims=True)
        first = jnp.min(jnp.where(vals == m, lane, N_EXPERTS), axis=-1, keepdims=True)
        sel = lane == first
        if j == 0:
            top0 = m
        e = jnp.exp(m - top0)
        den = den + e
        gates = jnp.where(sel, e, gates)
        vals = jnp.where(sel, -jnp.inf, vals)
    gate_ref[...] = jnp.where(gates >= 0.0, gates / den, -1.0)


def _router(x, gain, mods, w_router, b_router):
    row = pl.BlockSpec((ROUTER_TILE, D), lambda i: (i, 0))
    return pl.pallas_call(
        _router_kernel,
        out_shape=[jax.ShapeDtypeStruct((N_TOK, D), F32), jax.ShapeDtypeStruct((N_TOK, N_EXPERTS), F32)],
        grid=(N_TOK // ROUTER_TILE,),
        in_specs=[row, pl.BlockSpec((1, D), lambda i: (0, 0)), _mods_spec(ROUTER_TILE),
                  pl.BlockSpec((D, N_EXPERTS), lambda i: (0, 0)), pl.BlockSpec((1, N_EXPERTS), lambda i: (0, 0))],
        out_specs=[row, pl.BlockSpec((ROUTER_TILE, N_EXPERTS), lambda i: (i, 0))],
        compiler_params=_params(("parallel",)),
        name="router",
    )(x, gain.reshape(1, D), mods, w_router, b_router.reshape(1, N_EXPERTS))


def _row_copy(src_hbm, src_row, dst, dst_row, sem):
    return pltpu.make_async_copy(src_hbm.at[pl.ds(src_row, 1)], dst.at[pl.ds(dst_row, 1)], sem)


def _invert_kernel(pos_ref, tok_ref):
    def clear(i, carry):
        tok_ref[i] = 0
        return carry

    lax.fori_loop(0, MOE_ROWS, clear, 0, unroll=8)

    def place(a, carry):
        tok_ref[pos_ref[a]] = a // TOP_K
        return carry

    lax.fori_loop(0, N_TOK * TOP_K, place, 0, unroll=8)


def _invert(pos4):
    return pl.pallas_call(
        _invert_kernel,
        out_shape=jax.ShapeDtypeStruct((MOE_ROWS,), jnp.int32),
        in_specs=[pl.BlockSpec(memory_space=pltpu.SMEM)],
        out_specs=pl.BlockSpec(memory_space=pltpu.SMEM),
        name="invert",
    )(pos4)


def _experts_kernel(te_ref, nt_ref, tok_ref, h_hbm, wgu_ref, bgu_ref, wd_ref, bd_ref, y_ref,
                    xbuf, sem, wgu_bf, wd_bf):
    t = pl.program_id(0)

    @pl.when(t < nt_ref[0])
    def _():
        base = t * MOE_TILE

        def issue(r, carry):
            _row_copy(h_hbm, tok_ref[base + r], xbuf, r, sem).start()
            return carry

        lax.fori_loop(0, MOE_TILE, issue, 0)

        @pl.when((t == 0) | (te_ref[t] != te_ref[jnp.maximum(t - 1, 0)]))
        def _():
            wgu_bf[...] = wgu_ref[...].astype(BF16)
            wd_bf[...] = wd_ref[...].astype(BF16)

        def wait(r, carry):
            _row_copy(h_hbm, 0, xbuf, r, sem).wait()
            return carry

        lax.fori_loop(0, MOE_TILE, wait, 0)

        hid = jnp.dot(xbuf[...].astype(BF16), wgu_bf[...], preferred_element_type=F32) + bgu_ref[...]
        glu = jnp.minimum(hid[:, :D_EXPERT], SWIGLU_LIMIT)
        lin = jnp.clip(hid[:, D_EXPERT:], -SWIGLU_LIMIT, SWIGLU_LIMIT)
        act = glu * _sigmoid(SWIGLU_ALPHA * glu) * (lin + 1.0)
        y_ref[...] = jnp.dot(act.astype(BF16), wd_bf[...], preferred_element_type=F32) + bd_ref[...]

    @pl.when(t >= nt_ref[0])
    def _():
        y_ref[...] = jnp.zeros_like(y_ref)


def _experts(h, tile_expert, n_tiles, token_of, layer, w_gu, b_gu, w_down, b_down):
    weight = lambda shape: pl.BlockSpec((None, None) + shape, lambda t, te, nt, tok: (layer, te[t], 0, 0))
    grid_spec = pltpu.PrefetchScalarGridSpec(
        num_scalar_prefetch=3,
        grid=(MOE_TILES,),
        in_specs=[pl.BlockSpec(memory_space=pl.ANY),
                  weight((D, 2 * D_EXPERT)), weight((1, 2 * D_EXPERT)), weight((D_EXPERT, D)), weight((1, D))],
        out_specs=pl.BlockSpec((MOE_TILE, D), lambda t, te, nt, tok: (t, 0)),
        scratch_shapes=[pltpu.VMEM((MOE_TILE, D), F32), pltpu.SemaphoreType.DMA(()),
                        pltpu.VMEM((D, 2 * D_EXPERT), BF16), pltpu.VMEM((D_EXPERT, D), BF16)],
    )
    return pl.pallas_call(
        _experts_kernel,
        out_shape=jax.ShapeDtypeStruct((MOE_ROWS, D), F32),
        grid_spec=grid_spec,
        compiler_params=_params(("arbitrary",)),
        name="experts",
    )(tile_expert, n_tiles, token_of, h, w_gu, b_gu.reshape(DEPTH, N_EXPERTS, 1, 2 * D_EXPERT),
      w_down, b_down.reshape(DEPTH, N_EXPERTS, 1, D))


def _combine_kernel(pos_ref, y_hbm, gate_ref, x_ref, m_ref, o_ref, buf, sem):
    base = pl.program_id(0) * COMBINE_TILE * TOP_K

    def issue(r, carry):
        for j in range(TOP_K):
            _row_copy(y_hbm, pos_ref[base + r * TOP_K + j], buf.at[j], r, sem).start()
        return carry

    lax.fori_loop(0, COMBINE_TILE, issue, 0)

    def wait(r, carry):
        for j in range(TOP_K):
            _row_copy(y_hbm, 0, buf.at[j], r, sem).wait()
        return carry

    lax.fori_loop(0, COMBINE_TILE, wait, 0)
    gate = gate_ref[...]
    moe = ((buf[0] * gate[:, 0:1] + buf[1] * gate[:, 1:2]) + (buf[2] * gate[:, 2:3] + buf[3] * gate[:, 3:4]))
    o_ref[...] = x_ref[...] + m_ref[5] * moe


def _combine(y, pos4, gate4, x, mods):
    row = pl.BlockSpec((COMBINE_TILE, D), lambda i, pos: (i, 0))
    per = ROW_BLOCK // COMBINE_TILE
    grid_spec = pltpu.PrefetchScalarGridSpec(
        num_scalar_prefetch=1,
        grid=(N_TOK // COMBINE_TILE,),
        in_specs=[pl.BlockSpec(memory_space=pl.ANY), pl.BlockSpec((COMBINE_TILE, LANES), lambda i, pos: (i, 0)), row,
                  pl.BlockSpec((None, N_MOD, 1, D), lambda i, pos: (_cond_row(i // per), 0, 0, 0))],
        out_specs=row,
        scratch_shapes=[pltpu.VMEM((TOP_K, COMBINE_TILE, D), F32), pltpu.SemaphoreType.DMA(())],
    )
    return pl.pallas_call(
        _combine_kernel,
        out_shape=jax.ShapeDtypeStruct((N_TOK, D), F32),
        grid_spec=grid_spec,
        compiler_params=_params(("arbitrary",)),
        name="combine",
    )(pos4, y, gate4, x, mods)


def _moe_layer(x, gain, mods, layer, w_router, b_router, w_gu, b_gu, w_down, b_down):
    h, gates = _router(x, gain, mods, w_router, b_router)
    sel = gates >= 0.0
    sel_i = sel.astype(jnp.int32)
    rank = jnp.cumsum(sel_i, axis=0) - sel_i
    count = jnp.sum(sel_i, axis=0)
    padded = ((count + MOE_TILE - 1) // MOE_TILE) * MOE_TILE
    group_end = jnp.cumsum(padded)
    pos = group_end[None, :] - padded[None, :] + rank
    slot = jnp.cumsum(sel_i, axis=1) - 1
    pick = [sel & (slot == j) for j in range(TOP_K)]
    pos4 = jnp.stack([jnp.sum(jnp.where(m, pos, 0), axis=1) for m in pick], axis=1).astype(jnp.int32)
    gate4 = jnp.stack([jnp.sum(jnp.where(m, gates, 0.0), axis=1) for m in pick], axis=1)
    gate4 = jnp.pad(gate4, ((0, 0), (0, LANES - TOP_K)))
    pos4 = pos4.reshape(-1)
    n_tiles = (group_end[-1] // MOE_TILE).astype(jnp.int32)
    tile_start = jnp.minimum(jnp.arange(MOE_TILES, dtype=jnp.int32) * MOE_TILE, group_end[-1] - 1)
    tile_expert = jnp.sum((group_end[None, :] <= tile_start[:, None]).astype(jnp.int32), axis=1)
    tile_expert = jnp.minimum(tile_expert, N_EXPERTS - 1).astype(jnp.int32)
    y = _experts(h, tile_expert, n_tiles.reshape(1), _invert(pos4), layer, w_gu, b_gu, w_down, b_down)
    return _combine(y, pos4, gate4, x, mods)


def _rwkv_layer(x, gain, mods, state, p):
    (mu, w_rkv, w_out, w0, w1, w2, a0, a1, a2, g1, g2, k_k, k_a, r_k, lnx_w, lnx_b) = p
    h, xx = _normmod(x, gain, mods, with_xx=True)
    mu = mu.reshape(N_MOD, 1, D)
    rkv = _proj3(h, w_rkv, xx, mu[jnp.array([0, 2, 3])])
    wl = _lora(h, xx, mu[jnp.array([1, 1])], w1, w2, w0.reshape(2, 1, D), "tanh")
    al = _lora(h, xx, mu[jnp.array([4, 4])], a1, a2, a0.reshape(2, 1, D), "none")
    g = _lora(h, xx, mu[5:6], g1[None], g2[None], None, "sigmoid")[0]
    vecs = (k_k, k_a, r_k, lnx_w, lnx_b)
    z_p, s_ctx = _wkv(rkv, wl, al, None, *vecs, batch=BATCH, seq=SEQ, row_block0=0, emit_state=True)
    (z_s,) = _wkv(rkv, wl, al, state, *vecs, batch=DEC_BATCH, seq=DEC_SEQ,
                  row_block0=N_PROMPT // DEC_SEQ, emit_state=False)
    z = jnp.concatenate([z_p, z_s], axis=0)
    return _mmres(z, w_out, x, mods, g=g), s_ctx


def _na_layer(x, gain, mods, ck, cv, p):
    w_qkv, w_out, q_norm, k_norm, rpb = p
    (h,) = _normmod(x, gain, mods, with_xx=False)
    qkv = _proj3(h, w_qkv)
    o_p, k_p = _attention_prompt(qkv, q_norm, k_norm)
    o_s = _attention_latent(qkv, q_norm, k_norm, ck.reshape(DEC_BATCH, PAST_LEN, D),
                            cv.reshape(DEC_BATCH, PAST_LEN, D), _column_bias_table(rpb))
    o = jnp.concatenate([o_p, o_s], axis=0)
    new_k = k_p.reshape(BATCH, SEQ, HEADS, HEAD)
    new_v = qkv[2, :N_PROMPT].reshape(BATCH, SEQ, HEADS, HEAD)
    return _mmres(o, w_out, x, mods), new_k, new_v


def kernel(x_prompt, x_sample, c, c_ctx, state_wkv, cache_k, cache_v, norm_mix, norm_ffn, w_mod, b_mod, rw_mu, rw_w_rkv, rw_w_out, rw_w0, rw_w1, rw_w2, rw_a0, rw_a1, rw_a2, rw_g1, rw_g2, rw_k_k, rw_k_a, rw_r_k, rw_lnx_w, rw_lnx_b, pool_w, pool_scale, na_w_qkv, na_w_out, na_q_norm, na_k_norm, na_rpb, moe_w_router, moe_b_router, moe_w_gu, moe_b_gu, moe_w_down, moe_b_down):
    x = jnp.concatenate([x_prompt.reshape(N_PROMPT, D), x_sample.reshape(N_LATENT, D)], axis=0)
    cond = jnp.concatenate([c_ctx[None, :], c, jnp.zeros((COND_ROWS - 1 - DEC_BATCH, D), F32)], axis=0)
    mods = _adaln(cond, w_mod, b_mod)
    new_wkv, new_k, new_v = [], [], []
    for i in range(DEPTH):
        kind, slot = i % N_MIXERS, i // N_MIXERS
        if kind == 0:
            rw = (rw_mu[slot], rw_w_rkv[slot], rw_w_out[slot], rw_w0[slot], rw_w1[slot], rw_w2[slot],
                  rw_a0[slot], rw_a1[slot], rw_a2[slot], rw_g1[slot], rw_g2[slot], rw_k_k[slot],
                  rw_k_a[slot], rw_r_k[slot], rw_lnx_w[slot], rw_lnx_b[slot])
            x, s_ctx = _rwkv_layer(x, norm_mix[i], mods[i], state_wkv[:, slot], rw)
            new_wkv.append(s_ctx)
        elif kind == 1:
            x = _pool_layer(x, norm_mix[i], mods[i], pool_w[slot], pool_scale[slot])
        else:
            na = (na_w_qkv[slot], na_w_out[slot], na_q_norm[slot], na_k_norm[slot], na_rpb[slot])
            x, k_p, v_p = _na_layer(x, norm_mix[i], mods[i], cache_k[:, slot], cache_v[:, slot], na)
            new_k.append(k_p)
            new_v.append(v_p)
        x = _moe_layer(x, norm_ffn[i], mods[i], i, moe_w_router[i], moe_b_router[i], moe_w_gu, moe_b_gu,
                       moe_w_down, moe_b_down)
    y_prompt = x[:N_PROMPT].reshape(BATCH, SEQ, D)
    y_sample = x[N_PROMPT:].reshape(DEC_BATCH, DEC_SEQ, D)
    return (y_prompt, y_sample, jnp.stack(new_wkv, axis=1), jnp.stack(new_k, axis=1), jnp.stack(new_v, axis=1))
```

```python
import functools

import jax
import jax.numpy as jnp
import numpy as np
from jax import lax
from jax.experimental import pallas as pl
from jax.experimental.pallas import tpu as pltpu

F32 = jnp.float32
BF16 = jnp.bfloat16

D = 1024
BATCH, SEQ = 16, 256
DEC_BATCH, DEC_SEQ = 4, 1024
DEPTH = 4
PAST_LEN = 512
GRID_W = 64
N_MIXERS = 3
N_MOD = 6
NORM_EPS = 1e-6
HEAD = 64
HEADS = D // HEAD
RW_GN_EPS = 64e-5
POOL_WINDOWS = (2, 4, 8, 16)
POOL_GROUP = D // len(POOL_WINDOWS)
NA_WIN_ROWS, NA_WIN_COLS = 8, 16
N_EXPERTS, TOP_K = 32, 4
D_EXPERT = D
SWIGLU_LIMIT, SWIGLU_ALPHA = 7.0, 1.702

N_PROMPT = BATCH * SEQ
N_LATENT = DEC_BATCH * DEC_SEQ
N_TOK = N_PROMPT + N_LATENT
COND_ROWS = 8
ROW_BLOCK = 1024
PROMPT_BLOCKS = N_PROMPT // ROW_BLOCK
LANES = 128
SUBLANES = 8
HEADS_PER_STEP = LANES // HEAD
CHUNK = 64
WKV_GROUP = 4
WKV_PAIRS = 2
MOE_TILE = 256
MOE_ROWS = N_TOK * TOP_K + N_EXPERTS * MOE_TILE
MOE_TILES = MOE_ROWS // MOE_TILE
MOE_COL = 256
COMBINE_TILE = 256
NEG_BIG = -1e30
VMEM_LIMIT = 56 * 1024 * 1024


def _cond_row(block_1024):
    return jnp.maximum(block_1024 - (PROMPT_BLOCKS - 1), 0)


def _mods_spec(rows_per_block):
    per = ROW_BLOCK // rows_per_block
    return pl.BlockSpec((None, N_MOD, 1, D), lambda *ids: (_cond_row(ids[-1] // per), 0, 0, 0))


def _params(sem):
    return pltpu.CompilerParams(dimension_semantics=sem, vmem_limit_bytes=VMEM_LIMIT)


def _bdot(a, b):
    return jnp.dot(a.astype(BF16), b.astype(BF16), preferred_element_type=F32)


def _bdot_nt(a, b):
    return lax.dot_general(a.astype(BF16), b.astype(BF16), (((1,), (1,)), ((), ())),
                           preferred_element_type=F32)


def _bdot_tn(a, b):
    return lax.dot_general(a.astype(BF16), b.astype(BF16), (((0,), (0,)), ((), ())),
                           preferred_element_type=F32)


def _split3(x):
    hi = x.astype(BF16)
    r1 = x - hi.astype(F32)
    mid = r1.astype(BF16)
    lo = (r1 - mid.astype(F32)).astype(BF16)
    return hi, mid, lo


def _sigmoid(x):
    return 1.0 / (1.0 + jnp.exp(-x))


def _rms_mod(x, gain, shift, scale):
    y = x * lax.rsqrt(jnp.mean(x * x, axis=-1, keepdims=True) + NORM_EPS)
    return (y * gain) * (1.0 + scale) + shift


def _adaln_kernel(c_ref, w_ref, b_ref, o_ref):
    c = c_ref[...]
    s = c * _sigmoid(c)
    s_hi = s.astype(BF16)
    s_lo = (s - s_hi.astype(F32)).astype(BF16)
    w = w_ref[...]
    w_hi = w.astype(BF16)
    w_lo = (w - w_hi.astype(F32)).astype(BF16)
    o_ref[...] = (jnp.dot(s_hi, w_hi, preferred_element_type=F32) + jnp.dot(s_lo, w_hi, preferred_element_type=F32)
                  + jnp.dot(s_hi, w_lo, preferred_element_type=F32) + b_ref[...])


def _adaln(cond, w_mod, b_mod):
    out = pl.pallas_call(
        _adaln_kernel,
        out_shape=jax.ShapeDtypeStruct((DEPTH, COND_ROWS, N_MOD * D), F32),
        grid=(DEPTH, N_MOD),
        in_specs=[pl.BlockSpec((COND_ROWS, D), lambda l, j: (0, 0)),
                  pl.BlockSpec((None, D, D), lambda l, j: (l, 0, j)),
                  pl.BlockSpec((None, 1, D), lambda l, j: (l, 0, j))],
        out_specs=pl.BlockSpec((None, COND_ROWS, D), lambda l, j: (l, 0, j)),
        compiler_params=_params(("parallel", "parallel")),
        name="adaln",
    )(cond, w_mod, b_mod.reshape(DEPTH, 1, N_MOD * D))
    return out.reshape(DEPTH, COND_ROWS, N_MOD, 1, D)


def _normmod_kernel(x_ref, g_ref, m_ref, h_ref, *xx_ref, shift_idx, scale_idx):
    h = _rms_mod(x_ref[...], g_ref[...], m_ref[shift_idx], m_ref[scale_idx])
    h_ref[...] = h
    if xx_ref:
        seq = jnp.where(pl.program_id(0) < PROMPT_BLOCKS, SEQ, DEC_SEQ)
        t = lax.broadcasted_iota(jnp.int32, (ROW_BLOCK, 1), 0) & (seq - 1)
        prev = jnp.where(t == 0, 0.0, pltpu.roll(h, 1, 0))
        nxt = jnp.where(t == seq - 1, 0.0, pltpu.roll(h, ROW_BLOCK - 1, 0))
        xx_ref[0][...] = 0.5 * (prev + nxt) - h


def _normmod(x, gain, mods, with_xx):
    n_out = 2 if with_xx else 1
    row = pl.BlockSpec((ROW_BLOCK, D), lambda i: (i, 0))
    outs = pl.pallas_call(
        functools.partial(_normmod_kernel, shift_idx=0, scale_idx=1),
        out_shape=[jax.ShapeDtypeStruct((N_TOK, D), F32)] * n_out,
        grid=(N_TOK // ROW_BLOCK,),
        in_specs=[row, pl.BlockSpec((1, D), lambda i: (0, 0)), _mods_spec(ROW_BLOCK)],
        out_specs=[row] * n_out,
        compiler_params=_params(("parallel",)),
        name="normmod",
    )(x, gain.reshape(1, D), mods)
    return outs


PROJ_TILE = 512


def _proj3_kernel(*refs, mix):
    if mix:
        h_ref, xx_ref, mu_ref, w_ref, o_ref, wbf_ref = refs
    else:
        h_ref, w_ref, o_ref, wbf_ref = refs

    @pl.when(pl.program_id(1) == 0)
    def _():
        wbf_ref[...] = w_ref[...].astype(BF16)

    x = h_ref[...]
    if mix:
        x = x + xx_ref[...] * mu_ref[...]
    o_ref[...] = jnp.dot(x.astype(BF16), wbf_ref[...], preferred_element_type=F32)


def _proj3(h, w, xx=None, mu=None):
    mix = xx is not None
    row = pl.BlockSpec((PROJ_TILE, D), lambda j, i: (i, 0))
    if mix:
        ins = [h, xx, mu, w]
        specs = [row, row, pl.BlockSpec((None, 1, D), lambda j, i: (j, 0, 0)),
                 pl.BlockSpec((None, D, D), lambda j, i: (j, 0, 0))]
    else:
        ins = [h, w]
        specs = [row, pl.BlockSpec((D, D), lambda j, i: (0, j))]
    return pl.pallas_call(
        functools.partial(_proj3_kernel, mix=mix),
        out_shape=jax.ShapeDtypeStruct((3, N_TOK, D), F32),
        grid=(3, N_TOK // PROJ_TILE),
        in_specs=specs,
        out_specs=pl.BlockSpec((None, PROJ_TILE, D), lambda j, i: (j, i, 0)),
        scratch_shapes=[pltpu.VMEM((D, D), BF16)],
        compiler_params=_params(("arbitrary", "arbitrary")),
        name="proj3",
    )(*ins)


def _lora_kernel(h_ref, xx_ref, mu_ref, a_ref, b_ref, *rest, n, act, has_bias):
    if has_bias:
        bias_ref, o_ref = rest
    else:
        (o_ref,) = rest
    h = h_ref[...]
    xx = xx_ref[...]
    for j in range(n):
        x = h + xx * mu_ref[j]
        t = _bdot(x, a_ref[j])
        if act == "tanh":
            t = jnp.tanh(t)
        elif act == "sigmoid":
            t = _sigmoid(t)
        o = _bdot(t, b_ref[j])
        if has_bias:
            o = o + bias_ref[j]
        o_ref[j] = o


def _lora(h, xx, mu, a, b, bias, act):
    n, _, r = a.shape
    row = pl.BlockSpec((PROJ_TILE, D), lambda i: (i, 0))
    full = lambda shape: pl.BlockSpec(shape, lambda i: (0,) * len(shape))
    ins = [h, xx, mu, a, b]
    specs = [row, row, full((n, 1, D)), full((n, D, r)), full((n, r, D))]
    if bias is not None:
        ins.append(bias)
        specs.append(full((n, 1, D)))
    return pl.pallas_call(
        functools.partial(_lora_kernel, n=n, act=act, has_bias=bias is not None),
        out_shape=jax.ShapeDtypeStruct((n, N_TOK, D), F32),
        grid=(N_TOK // PROJ_TILE,),
        in_specs=specs,
        out_specs=pl.BlockSpec((n, PROJ_TILE, D), lambda i: (0, i, 0)),
        compiler_params=_params(("parallel",)),
        name="lora_" + act,
    )(*ins)


def _mmres_kernel(*refs, with_g, gate_idx):
    if with_g:
        z_ref, g_ref, w_ref, x_ref, m_ref, o_ref, wbf_ref = refs
    else:
        z_ref, w_ref, x_ref, m_ref, o_ref, wbf_ref = refs

    @pl.when(pl.program_id(0) == 0)
    def _():
        wbf_ref[...] = w_ref[...].astype(BF16)

    z = z_ref[...]
    if with_g:
        z = z * g_ref[...]
    o_ref[...] = x_ref[...] + m_ref[gate_idx] * jnp.dot(z.astype(BF16), wbf_ref[...],
                                                        preferred_element_type=F32)


def _mmres(z, w, x, mods, g=None):
    row = pl.BlockSpec((PROJ_TILE, D), lambda i: (i, 0))
    ins, specs = [z], [row]
    if g is not None:
        ins.append(g)
        specs.append(row)
    ins += [w, x, mods]
    specs += [pl.BlockSpec((D, D), lambda i: (0, 0)), row, _mods_spec(PROJ_TILE)]
    return pl.pallas_call(
        functools.partial(_mmres_kernel, with_g=g is not None, gate_idx=2),
        out_shape=jax.ShapeDtypeStruct((N_TOK, D), F32),
        grid=(N_TOK // PROJ_TILE,),
        in_specs=specs,
        out_specs=row,
        scratch_shapes=[pltpu.VMEM((D, D), BF16)],
        compiler_params=_params(("arbitrary",)),
        name="mmres",
    )(*ins)


def _block_diag(x, left):
    return jnp.concatenate([jnp.where(left, x, 0.0), jnp.where(left, 0.0, x)], axis=0)


def _wkvp_kernel(*refs, seq, has_s0, emit_state):
    it = iter(refs)
    rkv_ref, wl_ref, al_ref = next(it), next(it), next(it)
    s0_ref = next(it) if has_s0 else None
    kk_ref, ka_ref, rk_ref, lw_ref, lb_ref = next(it), next(it), next(it), next(it), next(it)
    z_ref = next(it)
    sf_ref = next(it) if emit_state else None
    w2_ref, of_ref, ry_ref, pc_ref, y_ref = next(it), next(it), next(it), next(it), next(it)
    n_chunks = seq // CHUNK
    P2 = 2 * HEAD

    ri = lax.broadcasted_iota(jnp.int32, (CHUNK, CHUNK), 0)
    ci = lax.broadcasted_iota(jnp.int32, (CHUNK, CHUNK), 1)
    tri_bf = ((ri >= ci).astype(BF16), (ri <= ci).astype(BF16))
    rp = lax.broadcasted_iota(jnp.int32, (CHUNK, P2), 0)
    lane = lax.broadcasted_iota(jnp.int32, (CHUNK, P2), 1)
    cp = lane & (HEAD - 1)
    left = lane < HEAD
    eye = (rp == cp).astype(F32)
    blk16 = (rp // 16) == (cp // 16)
    blk32 = (rp // 32) == (cp // 32)
    mask2 = (jnp.concatenate([rp > cp, rp >= cp], axis=0), jnp.concatenate([rp < cp, rp <= cp], axis=0))
    left2 = jnp.concatenate([left, left], axis=0)
    same_head = left2 == (lax.broadcasted_iota(jnp.int32, (2 * CHUNK, P2), 0) < HEAD)
    k_k, k_a, r_k = kk_ref[...], ka_ref[...], rk_ref[...]
    heads = tuple(slice(hh * HEAD, (hh + 1) * HEAD) for hh in range(WKV_PAIRS * 2))
    lanes = tuple(slice(q * P2, (q + 1) * P2) for q in range(WKV_PAIRS))
    bd = lambda x: _block_diag(x, left)

    def prepare(g, carry):
        ch = []
        for j in range(WKV_GROUP):
            cc = g * WKV_GROUP + j
            rows = pl.ds(pl.multiple_of(cc * CHUNK, CHUNK), CHUNK)
            r2, k2, v2 = rkv_ref[0, rows, :], rkv_ref[1, rows, :], rkv_ref[2, rows, :]
            kk2 = k2 * k_k
            kk2 = jnp.concatenate(
                [kk2[:, sl] / jnp.maximum(jnp.sqrt(jnp.sum(kk2[:, sl] * kk2[:, sl], axis=-1, keepdims=True)), 1e-12)
                 for sl in heads], axis=1)
            for d in range(2):
                w_in = -wl_ref[d, rows, :]
                softplus = jnp.maximum(w_in, 0.0) + jnp.log(1.0 + jnp.exp(-jnp.abs(w_in)))
                logdec = -jnp.exp(-softplus - 0.5)
                a2 = _sigmoid(al_ref[d, rows, :])
                hi, mid, lo = _split3(logdec)
                lp = (jnp.dot(tri_bf[d], hi, preferred_element_type=F32)
                      + jnp.dot(tri_bf[d], mid, preferred_element_type=F32)
                      + jnp.dot(tri_bf[d], lo, preferred_element_type=F32))
                lp_end = lp[CHUNK - 1:CHUNK, :] if d == 0 else lp[0:1, :]
                e_neg, p_end = jnp.exp(-lp), jnp.exp(lp_end)
                kt2 = kk2 * jnp.exp(lp - logdec)
                rt2 = r2 * jnp.exp(lp)
                ks2 = k2 * (1.0 + (a2 - 1.0) * k_a) * e_neg
                bs2 = kk2 * a2 * e_neg
                kh2, bh2 = ks2 * p_end, bs2 * p_end
                pc_ref[d, cc] = p_end
                for q, ql in enumerate(lanes):
                    ch.append(dict(cc=cc, rows=rows, d=d, q=q, ql=ql, kt=kt2[:, ql], rt=rt2[:, ql], ks=ks2[:, ql],
                                   bs=bs2[:, ql], kh=kh2[:, ql], bh=bh2[:, ql], v=v2[:, ql]))
        for c in ch:
            q2 = jnp.concatenate([c["kt"], c["rt"]], axis=0)
            c["a_k"] = jnp.where(mask2[c["d"]], _bdot_nt(q2, bd(c["ks"])), 0.0)
            c["a_b"] = jnp.where(mask2[c["d"]], _bdot_nt(q2, bd(c["bs"])), 0.0)
        for c in ch:
            c["av"] = _bdot(c["a_k"], bd(c["v"]))
        for c in ch:
            c["tri"] = c["a_b"][:CHUNK]
            d16 = jnp.where(blk16, c["tri"], 0.0)
            c["x"] = eye - d16
            c["p"] = _bdot(d16, bd(d16))
        for level in range(3):
            for c in ch:
                c["x"] = c["x"] + _bdot(c["x"], bd(c["p"]))
            if level < 2:
                for c in ch:
                    c["p"] = _bdot(c["p"], bd(c["p"]))
        for inner, outer in ((blk16, blk32), (blk32, None)):
            keep = (~inner) if outer is None else (outer & (~inner))
            for c in ch:
                c["t"] = _bdot(c["x"], bd(jnp.where(keep, c["tri"], 0.0)))
            for c in ch:
                c["x"] = c["x"] - _bdot(c["t"], bd(c["x"]))
        for c in ch:
            c["wu"] = _bdot(c["x"], jnp.concatenate([bd(c["kt"]), bd(c["av"][:CHUNK])], axis=1))
        for c in ch:
            wm, uv = c["wu"][:, :P2], c["wu"][:, P2:]
            corr = _bdot(c["a_b"][CHUNK:], jnp.concatenate([bd(wm), bd(uv)], axis=1))
            d, q, cc = c["d"], c["q"], c["cc"]
            ry_ref[d, q, cc] = (c["rt"] - corr[:, :P2]).astype(BF16)
            y_ref[d, c["rows"], c["ql"]] = c["av"][CHUNK:] - corr[:, P2:]
            w2_ref[d, q, cc] = jnp.where(same_head, _bdot_tn(wm, c["bh"]), 0.0).astype(BF16)
            full = _bdot_tn(jnp.concatenate([c["v"], -uv], axis=0), jnp.concatenate([c["kh"], c["bh"]], axis=0))
            of_ref[d, q, cc] = jnp.where(left, full[:HEAD], full[HEAD:])
        return carry

    lax.fori_loop(0, n_chunks // WKV_GROUP, prepare, 0)

    def advance(c, states):
        new = []
        for d in range(2):
            cc = c if d == 0 else n_chunks - 1 - c
            rows = pl.ds(pl.multiple_of(cc * CHUNK, CHUNK), CHUNK)
            p_end = pc_ref[d, cc]
            for q, ql in enumerate(lanes):
                s = states[d * WKV_PAIRS + q]
                y_ref[d, rows, ql] = y_ref[d, rows, ql] + _bdot_nt(ry_ref[d, q, cc], bd(s))
                new.append(s * p_end[:, ql] - _bdot(s, w2_ref[d, q, cc]) + of_ref[d, q, cc])
        return tuple(new)

    if has_s0:
        init = tuple(jnp.concatenate([s0_ref[d, 2 * q], s0_ref[d, 2 * q + 1]], axis=1)
                     for d in range(2) for q in range(WKV_PAIRS))
    else:
        init = tuple(jnp.zeros((HEAD, P2), F32) for _ in range(2 * WKV_PAIRS))
    final = lax.fori_loop(0, n_chunks, advance, init)

    if emit_state:
        for d in range(2):
            for q in range(WKV_PAIRS):
                s = final[d * WKV_PAIRS + q]
                sf_ref[d, 2 * q] = s[:, :HEAD]
                sf_ref[d, 2 * q + 1] = s[:, HEAD:]

    EP = 256
    lnx_w, lnx_b = lw_ref[...], lb_ref[...]

    def epilogue(i, carry):
        rows = pl.ds(pl.multiple_of(i * EP, EP), EP)
        r2, k2, v2 = rkv_ref[0, rows, :], rkv_ref[1, rows, :], rkv_ref[2, rows, :]
        y2 = y_ref[0, rows, :] + y_ref[1, rows, :]
        coef = 2.0 + (_sigmoid(al_ref[0, rows, :]) + _sigmoid(al_ref[1, rows, :]) - 2.0) * k_a
        rkr = r2 * k2 * coef * r_k
        yn, bonus = [], []
        for sl in heads:
            y = y2[:, sl]
            mean = jnp.mean(y, axis=-1, keepdims=True)
            var = jnp.mean(jnp.square(y - mean), axis=-1, keepdims=True)
            yn.append((y - mean) * lax.rsqrt(var + RW_GN_EPS))
            bonus.append(jnp.sum(rkr[:, sl], axis=-1, keepdims=True) * v2[:, sl])
        z_ref[rows, :] = jnp.concatenate(yn, axis=1) * lnx_w + lnx_b + jnp.concatenate(bonus, axis=1)
        return carry

    lax.fori_loop(0, seq // EP, epilogue, 0)


def _wkvp(rkv, wl, al, s0, k_k, k_a, r_k, lnx_w, lnx_b, *, batch, seq, row_block0, emit_state):
    has_s0 = s0 is not None
    width = WKV_PAIRS * 2 * HEAD
    steps = D // width
    n_chunks = seq // CHUNK
    tok = lambda lead: pl.BlockSpec((lead, seq, width), lambda b, p: (0, row_block0 + b, p))
    vec = pl.BlockSpec((1, width), lambda b, p: (0, p))
    st = pl.BlockSpec((None, 2, WKV_PAIRS * 2, HEAD, HEAD), lambda b, p: (b, 0, p, 0, 0))
    ins, specs = [rkv, wl, al], [tok(3), tok(2), tok(2)]
    if has_s0:
        ins.append(s0)
        specs.append(st)
    ins += [k_k.reshape(1, D), k_a.reshape(1, D), r_k.reshape(1, D), lnx_w.reshape(1, D), lnx_b.reshape(1, D)]
    specs += [vec] * 5
    out_shape = [jax.ShapeDtypeStruct((batch * seq, D), F32)]
    out_specs = [pl.BlockSpec((seq, width), lambda b, p: (b, p))]
    if emit_state:
        out_shape.append(jax.ShapeDtypeStruct((batch, 2, HEADS, HEAD, HEAD), F32))
        out_specs.append(st)
    return pl.pallas_call(
        functools.partial(_wkvp_kernel, seq=seq, has_s0=has_s0, emit_state=emit_state),
        out_shape=out_shape,
        grid=(batch, steps),
        in_specs=specs,
        out_specs=out_specs,
        scratch_shapes=[pltpu.VMEM((2, WKV_PAIRS, n_chunks, 2 * HEAD, 2 * HEAD), BF16),
                        pltpu.VMEM((2, WKV_PAIRS, n_chunks, HEAD, 2 * HEAD), F32),
                        pltpu.VMEM((2, WKV_PAIRS, n_chunks, CHUNK, 2 * HEAD), BF16),
                        pltpu.VMEM((2, n_chunks, 1, width), F32),
                        pltpu.VMEM((2, seq, width), F32)],
        compiler_params=_params(("parallel", "parallel")),
        name="wkv_%d" % seq,
    )(*ins)


def _pool_kernel(x_ref, g_ref, m_ref, w_ref, sc_ref, o_ref):
    x = x_ref[...]
    h = _rms_mod(x, g_ref[...], m_ref[0], m_ref[1])
    seq = jnp.where(pl.program_id(0) < PROMPT_BLOCKS, SEQ, DEC_SEQ)
    t = lax.broadcasted_iota(jnp.int32, (ROW_BLOCK, 1), 0) & (seq - 1)
    gate = m_ref[2]
    scale = sc_ref[...]
    for g, win in enumerate(POOL_WINDOWS):
        half = win // 2
        cols = slice(g * POOL_GROUP, (g + 1) * POOL_GROUP)
        hg = h[:, cols]
        up = lambda z, m: jnp.where(t + m <= seq - 1, pltpu.roll(z, ROW_BLOCK - m, 0), 0.0)
        down = lambda z, m: jnp.where(t - m >= 0, pltpu.roll(z, m, 0), 0.0)
        fwd = hg
        bwd = down(hg, 1)
        m = 1
        while m < half:
            fwd = fwd + up(fwd, m)
            bwd = bwd + down(bwd, m)
            m *= 2
        count = (jnp.minimum(t + half - 1, seq - 1) - jnp.maximum(t - half, 0) + 1).astype(F32)
        pooled = (fwd + bwd) / count - hg
        mixed = _bdot(pooled, w_ref[g]) * scale[:, cols]
        o_ref[:, cols] = x[:, cols] + gate[:, cols] * mixed


def _pool_layer(x, gain, mods, w_pool, scale):
    row = pl.BlockSpec((ROW_BLOCK, D), lambda i: (i, 0))
    n_g = len(POOL_WINDOWS)
    return pl.pallas_call(
        _pool_kernel,
        out_shape=jax.ShapeDtypeStruct((N_TOK, D), F32),
        grid=(N_TOK // ROW_BLOCK,),
        in_specs=[row, pl.BlockSpec((1, D), lambda i: (0, 0)), _mods_spec(ROW_BLOCK),
                  pl.BlockSpec((n_g, POOL_GROUP, POOL_GROUP), lambda i: (0, 0, 0)),
                  pl.BlockSpec((1, D), lambda i: (0, 0))],
        out_specs=row,
        compiler_params=_params(("parallel",)),
        name="pool",
    )(x, gain.reshape(1, D), mods, w_pool, scale.reshape(1, D))


NA_ROWS = DEC_SEQ // GRID_W
NA_WIN_R = min(NA_WIN_ROWS, NA_ROWS)


def _head_rms(x, gain):
    return x * lax.rsqrt(jnp.mean(x * x, axis=-1, keepdims=True) + NORM_EPS) * gain


def _attn_prompt_kernel(q_ref, k_ref, v_ref, qg_ref, kg_ref, o_ref, kn_ref):
    outs, kns = [], []
    for hh in range(HEADS_PER_STEP):
        sl = slice(hh * HEAD, (hh + 1) * HEAD)
        q = _head_rms(q_ref[:, sl], qg_ref[...]) * (HEAD ** -0.5)
        k = _head_rms(k_ref[:, sl], kg_ref[...])
        s = _bdot_nt(q, k)
        p = jnp.exp(s - jnp.max(s, axis=-1, keepdims=True))
        outs.append(_bdot(p, v_ref[:, sl]) / jnp.sum(p, axis=-1, keepdims=True))
        kns.append(k)
    o_ref[...] = jnp.concatenate(outs, axis=1)
    kn_ref[...] = jnp.concatenate(kns, axis=1)


def _attn_latent_kernel(q_ref, k_ref, v_ref, qg_ref, kg_ref, ck_ref, cv_ref, bias_ref, o_ref):
    outs = []
    for hh in range(HEADS_PER_STEP):
        sl = slice(hh * HEAD, (hh + 1) * HEAD)
        q = _head_rms(q_ref[:, sl], qg_ref[...]) * (HEAD ** -0.5)
        k = _head_rms(k_ref[:, sl], kg_ref[...])
        v = v_ref[:, sl]
        s_ctx = _bdot_nt(q, ck_ref[:, sl])
        m_ctx = jnp.max(s_ctx, axis=-1, keepdims=True)
        p_ctx = jnp.exp(s_ctx - m_ctx)
        l_ctx = jnp.sum(p_ctx, axis=-1, keepdims=True)
        o_ctx = _bdot(p_ctx, cv_ref[:, sl])
        rows_out = []
        for qr in range(NA_ROWS):
            r0 = min(max(qr - NA_WIN_R // 2, 0), NA_ROWS - NA_WIN_R)
            qs = slice(qr * GRID_W, (qr + 1) * GRID_W)
            ks = slice(r0 * GRID_W, (r0 + NA_WIN_R) * GRID_W)
            bias = jnp.concatenate([bias_ref[hh, r0 + j - qr + NA_WIN_ROWS - 1] for j in range(NA_WIN_R)], axis=1)
            s = _bdot_nt(q[qs], k[ks]) + bias
            m = jnp.maximum(jnp.max(s, axis=-1, keepdims=True), m_ctx[qs])
            p = jnp.exp(s - m)
            w_ctx = jnp.exp(m_ctx[qs] - m)
            den = jnp.sum(p, axis=-1, keepdims=True) + l_ctx[qs] * w_ctx
            rows_out.append((_bdot(p, v[ks]) + o_ctx[qs] * w_ctx) / den)
        outs.append(jnp.concatenate(rows_out, axis=0))
    o_ref[...] = jnp.concatenate(outs, axis=1)


def _attention_prompt(qkv, q_gain, k_gain):
    pair = HEADS // HEADS_PER_STEP
    tok = lambda j: pl.BlockSpec((None, SEQ, LANES), lambda b, p: (j, b, p))
    gain = pl.BlockSpec((1, HEAD), lambda b, p: (0, 0))
    out = pl.BlockSpec((SEQ, LANES), lambda b, p: (b, p))
    return pl.pallas_call(
        _attn_prompt_kernel,
        out_shape=[jax.ShapeDtypeStruct((N_PROMPT, D), F32)] * 2,
        grid=(BATCH, pair),
        in_specs=[tok(0), tok(1), tok(2), gain, gain],
        out_specs=[out, out],
        compiler_params=_params(("parallel", "parallel")),
        name="attn_prompt",
    )(qkv, qkv, qkv, q_gain.reshape(1, HEAD), k_gain.reshape(1, HEAD))


def _attention_latent(qkv, q_gain, k_gain, ck, cv, bias):
    pair = HEADS // HEADS_PER_STEP
    row0 = N_PROMPT // DEC_SEQ
    gain = pl.BlockSpec((1, HEAD), lambda p, b: (0, 0))
    tok = lambda j: pl.BlockSpec((None, DEC_SEQ, LANES), lambda p, b: (j, row0 + b, p))
    ctx = pl.BlockSpec((None, PAST_LEN, LANES), lambda p, b: (b, 0, p))
    n_dr = 2 * NA_WIN_ROWS - 1
    return pl.pallas_call(
        _attn_latent_kernel,
        out_shape=jax.ShapeDtypeStruct((N_LATENT, D), F32),
        grid=(pair, DEC_BATCH),
        in_specs=[tok(0), tok(1), tok(2), gain, gain, ctx, ctx,
                  pl.BlockSpec((HEADS_PER_STEP, n_dr, GRID_W, GRID_W), lambda p, b: (p, 0, 0, 0))],
        out_specs=pl.BlockSpec((DEC_SEQ, LANES), lambda p, b: (b, p)),
        compiler_params=_params(("parallel", "parallel")),
        name="attn_latent",
    )(qkv, qkv, qkv, q_gain.reshape(1, HEAD), k_gain.reshape(1, HEAD), ck, cv, bias)


def _column_bias_table(rpb):
    n_dr, n_dc = rpb.shape[1], rpb.shape[2]
    span = 2 * GRID_W - 1
    left = GRID_W - NA_WIN_COLS
    g = jnp.pad(rpb, ((0, 0), (0, 0), (left, span - n_dc - left)))
    flat = jnp.broadcast_to(g[:, :, None, :], (HEADS, n_dr, GRID_W, span)).reshape(HEADS, n_dr, GRID_W * span)
    table = flat[:, :, GRID_W - 1:GRID_W - 1 + GRID_W * (span - 1)].reshape(HEADS, n_dr, GRID_W, span - 1)
    table = table[..., :GRID_W]
    col = np.arange(GRID_W)
    c0 = np.clip(col - NA_WIN_COLS // 2, 0, GRID_W - NA_WIN_COLS)
    inside = (col[None, :] >= c0[:, None]) & (col[None, :] < c0[:, None] + NA_WIN_COLS)
    return jnp.where(inside[None, None], table, NEG_BIG)


ROUTER_TILE = 512


def _router_kernel(x_ref, g_ref, m_ref, w_ref, b_ref, h_ref, gate_ref):
    h = _rms_mod(x_ref[...], g_ref[...], m_ref[3], m_ref[4])
    h_ref[...] = h
    h_hi = h.astype(BF16)
    h_lo = (h - h_hi.astype(F32)).astype(BF16)
    w = w_ref[...]
    w_hi = w.astype(BF16)
    w_lo = (w - w_hi.astype(F32)).astype(BF16)
    logits = (jnp.dot(h_hi, w_hi, preferred_element_type=F32) + jnp.dot(h_hi, w_lo, preferred_element_type=F32)
              + jnp.dot(h_lo, w_hi, preferred_element_type=F32) + b_ref[...])
    lane = lax.broadcasted_iota(jnp.int32, logits.shape, 1)
    vals = logits
    top0 = None
    den = 0.0
    gates = jnp.full(logits.shape, -1.0, F32)
    for j in range(TOP_K):
        m = jnp.max(vals, axis=-1, keepdims=True)
        first = jnp.min(jnp.where(vals == m, lane, N_EXPERTS), axis=-1, keepdims=True)
        sel = lane == first
        if j == 0:
            top0 = m
        e = jnp.exp(m - top0)
        den = den + e
        gates = jnp.where(sel, e, gates)
        vals = jnp.where(sel, -jnp.inf, vals)
    gate_ref[...] = jnp.where(gates >= 0.0, gates / den, -1.0)


def _router(x, gain, mods, w_router, b_router):
    row = pl.BlockSpec((ROUTER_TILE, D), lambda i: (i, 0))
    return pl.pallas_call(
        _router_kernel,
        out_shape=[jax.ShapeDtypeStruct((N_TOK, D), F32), jax.ShapeDtypeStruct((N_TOK, N_EXPERTS), F32)],
        grid=(N_TOK // ROUTER_TILE,),
        in_specs=[row, pl.BlockSpec((1, D), lambda i: (0, 0)), _mods_spec(ROUTER_TILE),
                  pl.BlockSpec((D, N_EXPERTS), lambda i: (0, 0)), pl.BlockSpec((1, N_EXPERTS), lambda i: (0, 0))],
        out_specs=[row, pl.BlockSpec((ROUTER_TILE, N_EXPERTS), lambda i: (i, 0))],
        compiler_params=_params(("parallel",)),
        name="router",
    )(x, gain.reshape(1, D), mods, w_router, b_router.reshape(1, N_EXPERTS))


def _row_copy(src_hbm, src_row, dst, dst_row, sem):
    return pltpu.make_async_copy(src_hbm.at[pl.ds(src_row, 1)], dst.at[pl.ds(dst_row, 1)], sem)


def _invert_kernel(pos_ref, dst_ref):
    def clear(i, carry):
        dst_ref[i] = 0
        return carry

    lax.fori_loop(0, MOE_ROWS, clear, 0, unroll=8)

    def place(n, carry):
        for j in range(TOP_K):
            dst_ref[pos_ref[n * TOP_K + j]] = j * N_TOK + n
        return carry

    lax.fori_loop(0, N_TOK, place, 0, unroll=2)


def _invert(pos4):
    return pl.pallas_call(
        _invert_kernel,
        out_shape=jax.ShapeDtypeStruct((MOE_ROWS,), jnp.int32),
        in_specs=[pl.BlockSpec(memory_space=pltpu.SMEM)],
        out_specs=pl.BlockSpec(memory_space=pltpu.SMEM),
        name="invert",
    )(pos4)


def _experts_kernel(te_ref, tv_ref, nt_ref, dst_ref, h_hbm, wgu_ref, bgu_ref, wd_ref, bd_ref, ys_hbm,
                    xbuf, ybuf, sem_in, sem_out, wgu_bf, wd_bf, act_ref):
    t = pl.program_id(0)
    n_tiles = nt_ref[0]
    slot = t % 2
    other = 1 - slot

    def gather_row(tile, r, to_slot):
        tok = dst_ref[tile * MOE_TILE + r] & (N_TOK - 1)
        return _row_copy(h_hbm, tok, xbuf.at[to_slot], r, sem_in.at[to_slot])

    def scatter_row(tile, r, from_slot):
        return _row_copy(ybuf.at[from_slot], r, ys_hbm, dst_ref[tile * MOE_TILE + r], sem_out.at[from_slot])

    def gather_wait(s):
        pltpu.make_async_copy(h_hbm.at[pl.ds(0, MOE_TILE)], xbuf.at[s], sem_in.at[s]).wait()

    def scatter_wait(s, rows):
        whole = pl.multiple_of((rows // SUBLANES) * SUBLANES, SUBLANES)

        @pl.when(whole > 0)
        def _():
            pltpu.make_async_copy(ybuf.at[s].at[pl.ds(0, whole)], ys_hbm.at[pl.ds(0, whole)], sem_out.at[s]).wait()

        def one(r, carry):
            _row_copy(ybuf.at[s], 0, ys_hbm, 0, sem_out.at[s]).wait()
            return carry

        lax.fori_loop(0, rows - whole, one, 0)

    @pl.when(t == 0)
    def _():
        def issue(r, carry):
            gather_row(0, r, 0).start()
            return carry

        lax.fori_loop(0, MOE_TILE, issue, 0)

    @pl.when((t >= 2) & (t < n_tiles))
    def _():
        scatter_wait(slot, tv_ref[t - 1])

    @pl.when(t < n_tiles)
    def _():
        @pl.when((t == 0) | (te_ref[t] != te_ref[jnp.maximum(t - 1, 0)]))
        def _():
            wgu_bf[...] = wgu_ref[...].astype(BF16)
            wd_bf[...] = wd_ref[...].astype(BF16)

        gather_wait(slot)
        nxt = jnp.minimum(t + 1, n_tiles - 1)
        prev, prev_rows = jnp.maximum(t - 1, 0), tv_ref[t]
        n_up, n_down = D_EXPERT // MOE_COL, D // MOE_COL
        per = MOE_TILE // (n_up + n_down)

        def issue_rows(block):
            for r in range(block * per, (block + 1) * per):
                gather_row(nxt, r, other).start()

                @pl.when(r < prev_rows)
                def _():
                    scatter_row(prev, r, other).start()

        x = xbuf[slot].astype(BF16)
        for c in range(n_up):
            cols = slice(c * MOE_COL, (c + 1) * MOE_COL)
            ups = slice(D_EXPERT + c * MOE_COL, D_EXPERT + (c + 1) * MOE_COL)
            glu = jnp.dot(x, wgu_bf[:, cols], preferred_element_type=F32) + bgu_ref[:, cols]
            lin = jnp.dot(x, wgu_bf[:, ups], preferred_element_type=F32) + bgu_ref[:, ups]
            glu = jnp.minimum(glu, SWIGLU_LIMIT)
            lin = jnp.clip(lin, -SWIGLU_LIMIT, SWIGLU_LIMIT)
            act_ref[:, cols] = (glu * _sigmoid(SWIGLU_ALPHA * glu) * (lin + 1.0)).astype(BF16)
            issue_rows(c)
        act = act_ref[...]
        for c in range(n_down):
            cols = slice(c * MOE_COL, (c + 1) * MOE_COL)
            ybuf[slot, :, cols] = jnp.dot(act, wd_bf[:, cols], preferred_element_type=F32) + bd_ref[:, cols]
            issue_rows(n_up + c)

    @pl.when(t == n_tiles - 1)
    def _():
        def issue(r, carry):
            scatter_row(t, r, slot).start()
            return carry

        lax.fori_loop(0, tv_ref[t + 1], issue, 0)
        gather_wait(other)
        scatter_wait(other, tv_ref[t])
        scatter_wait(slot, tv_ref[t + 1])


def _experts(h, tile_expert, tile_rows, n_tiles, dst, layer, w_gu, b_gu, w_down, b_down):
    weight = lambda shape: pl.BlockSpec((None, None) + shape, lambda t, te, tv, nt, dst: (layer, te[t], 0, 0))
    grid_spec = pltpu.PrefetchScalarGridSpec(
        num_scalar_prefetch=4,
        grid=(MOE_TILES,),
        in_specs=[pl.BlockSpec(memory_space=pl.ANY),
                  weight((D, 2 * D_EXPERT)), weight((1, 2 * D_EXPERT)), weight((D_EXPERT, D)), weight((1, D))],
        out_specs=pl.BlockSpec(memory_space=pl.ANY),
        scratch_shapes=[pltpu.VMEM((2, MOE_TILE, D), F32), pltpu.VMEM((2, MOE_TILE, D), F32),
                        pltpu.SemaphoreType.DMA((2,)), pltpu.SemaphoreType.DMA((2,)),
                        pltpu.VMEM((D, 2 * D_EXPERT), BF16), pltpu.VMEM((D_EXPERT, D), BF16),
                        pltpu.VMEM((MOE_TILE, D_EXPERT), BF16)],
    )
    return pl.pallas_call(
        _experts_kernel,
        out_shape=jax.ShapeDtypeStruct((TOP_K * N_TOK, D), F32),
        grid_spec=grid_spec,
        compiler_params=_params(("arbitrary",)),
        name="experts",
    )(tile_expert, tile_rows, n_tiles, dst, h, w_gu, b_gu.reshape(DEPTH, N_EXPERTS, 1, 2 * D_EXPERT),
      w_down, b_down.reshape(DEPTH, N_EXPERTS, 1, D))


def _combine_kernel(y0_ref, y1_ref, y2_ref, y3_ref, gate_ref, x_ref, m_ref, o_ref):
    gate = gate_ref[...]
    moe = ((y0_ref[...] * gate[:, 0:1] + y1_ref[...] * gate[:, 1:2])
           + (y2_ref[...] * gate[:, 2:3] + y3_ref[...] * gate[:, 3:4]))
    o_ref[...] = x_ref[...] + m_ref[5] * moe


def _combine(ys, gate4, x, mods):
    row = pl.BlockSpec((COMBINE_TILE, D), lambda i: (i, 0))
    blocks = N_TOK // COMBINE_TILE
    choice = lambda j: pl.BlockSpec((COMBINE_TILE, D), lambda i: (j * blocks + i, 0))
    return pl.pallas_call(
        _combine_kernel,
        out_shape=jax.ShapeDtypeStruct((N_TOK, D), F32),
        grid=(blocks,),
        in_specs=[choice(j) for j in range(TOP_K)]
        + [pl.BlockSpec((COMBINE_TILE, LANES), lambda i: (i, 0)), row, _mods_spec(COMBINE_TILE)],
        out_specs=row,
        compiler_params=_params(("parallel",)),
        name="combine",
    )(ys, ys, ys, ys, gate4, x, mods)


def _moe_layer(x, gain, mods, layer, w_router, b_router, w_gu, b_gu, w_down, b_down):
    h, gates = _router(x, gain, mods, w_router, b_router)
    sel = gates >= 0.0
    sel_i = sel.astype(jnp.int32)
    rank = jnp.cumsum(sel_i, axis=0) - sel_i
    count = jnp.sum(sel_i, axis=0)
    padded = ((count + MOE_TILE - 1) // MOE_TILE) * MOE_TILE
    group_end = jnp.cumsum(padded)
    pos = group_end[None, :] - padded[None, :] + rank
    slot = jnp.cumsum(sel_i, axis=1) - 1
    pick = [sel & (slot == j) for j in range(TOP_K)]
    pos4 = jnp.stack([jnp.sum(jnp.where(m, pos, 0), axis=1) for m in pick], axis=1).astype(jnp.int32)
    gate4 = jnp.stack([jnp.sum(jnp.where(m, gates, 0.0), axis=1) for m in pick], axis=1)
    gate4 = jnp.pad(gate4, ((0, 0), (0, LANES - TOP_K)))
    pos4 = pos4.reshape(-1)
    n_tiles = (group_end[-1] // MOE_TILE).astype(jnp.int32)
    tile_start = jnp.minimum(jnp.arange(MOE_TILES, dtype=jnp.int32) * MOE_TILE, group_end[-1] - 1)
    tile_expert = jnp.sum((group_end[None, :] <= tile_start[:, None]).astype(jnp.int32), axis=1)
    tile_expert = jnp.minimum(tile_expert, N_EXPERTS - 1).astype(jnp.int32)
    real_end = group_end - padded + count
    tile_rows = jnp.clip(real_end[tile_expert] - jnp.arange(MOE_TILES, dtype=jnp.int32) * MOE_TILE, 0, MOE_TILE)
    tile_rows = jnp.concatenate([jnp.zeros((1,), jnp.int32), tile_rows.astype(jnp.int32)])
    ys = _experts(h, tile_expert, tile_rows, n_tiles.reshape(1), _invert(pos4), layer, w_gu, b_gu, w_down, b_down)
    return _combine(ys, gate4, x, mods)


def _rwkv_layer(x, gain, mods, state, p):
    (mu, w_rkv, w_out, w0, w1, w2, a0, a1, a2, g1, g2, k_k, k_a, r_k, lnx_w, lnx_b) = p
    h, xx = _normmod(x, gain, mods, with_xx=True)
    mu = mu.reshape(N_MOD, 1, D)
    rkv = _proj3(h, w_rkv, xx, mu[jnp.array([0, 2, 3])])
    wl = _lora(h, xx, mu[jnp.array([1, 1])], w1, w2, w0.reshape(2, 1, D), "tanh")
    al = _lora(h, xx, mu[jnp.array([4, 4])], a1, a2, a0.reshape(2, 1, D), "none")
    g = _lora(h, xx, mu[5:6], g1[None], g2[None], None, "sigmoid")[0]
    vecs = (k_k, k_a, r_k, lnx_w, lnx_b)
    z_p, s_ctx = _wkvp(rkv, wl, al, None, *vecs, batch=BATCH, seq=SEQ, row_block0=0, emit_state=True)
    (z_s,) = _wkvp(rkv, wl, al, state, *vecs, batch=DEC_BATCH, seq=DEC_SEQ,
                  row_block0=N_PROMPT // DEC_SEQ, emit_state=False)
    z = jnp.concatenate([z_p, z_s], axis=0)
    return _mmres(z, w_out, x, mods, g=g), s_ctx


def _na_layer(x, gain, mods, ck, cv, p):
    w_qkv, w_out, q_norm, k_norm, rpb = p
    (h,) = _normmod(x, gain, mods, with_xx=False)
    qkv = _proj3(h, w_qkv)
    o_p, k_p = _attention_prompt(qkv, q_norm, k_norm)
    o_s = _attention_latent(qkv, q_norm, k_norm, ck.reshape(DEC_BATCH, PAST_LEN, D),
                            cv.reshape(DEC_BATCH, PAST_LEN, D), _column_bias_table(rpb))
    o = jnp.concatenate([o_p, o_s], axis=0)
    new_k = k_p.reshape(BATCH, SEQ, HEADS, HEAD)
    new_v = qkv[2, :N_PROMPT].reshape(BATCH, SEQ, HEADS, HEAD)
    return _mmres(o, w_out, x, mods), new_k, new_v


def kernel(x_prompt, x_sample, c, c_ctx, state_wkv, cache_k, cache_v, norm_mix, norm_ffn, w_mod, b_mod, rw_mu, rw_w_rkv, rw_w_out, rw_w0, rw_w1, rw_w2, rw_a0, rw_a1, rw_a2, rw_g1, rw_g2, rw_k_k, rw_k_a, rw_r_k, rw_lnx_w, rw_lnx_b, pool_w, pool_scale, na_w_qkv, na_w_out, na_q_norm, na_k_norm, na_rpb, moe_w_router, moe_b_router, moe_w_gu, moe_b_gu, moe_w_down, moe_b_down):
    x = jnp.concatenate([x_prompt.reshape(N_PROMPT, D), x_sample.reshape(N_LATENT, D)], axis=0)
    cond = jnp.concatenate([c_ctx[None, :], c, jnp.zeros((COND_ROWS - 1 - DEC_BATCH, D), F32)], axis=0)
    mods = _adaln(cond, w_mod, b_mod)
    new_wkv, new_k, new_v = [], [], []
    for i in range(DEPTH):
        kind, slot = i % N_MIXERS, i // N_MIXERS
        if kind == 0:
            rw = (rw_mu[slot], rw_w_rkv[slot], rw_w_out[slot], rw_w0[slot], rw_w1[slot], rw_w2[slot],
                  rw_a0[slot], rw_a1[slot], rw_a2[slot], rw_g1[slot], rw_g2[slot], rw_k_k[slot],
                  rw_k_a[slot], rw_r_k[slot], rw_lnx_w[slot], rw_lnx_b[slot])
            x, s_ctx = _rwkv_layer(x, norm_mix[i], mods[i], state_wkv[:, slot], rw)
            new_wkv.append(s_ctx)
        elif kind == 1:
            x = _pool_layer(x, norm_mix[i], mods[i], pool_w[slot], pool_scale[slot])
        else:
            na = (na_w_qkv[slot], na_w_out[slot], na_q_norm[slot], na_k_norm[slot], na_rpb[slot])
            x, k_p, v_p = _na_layer(x, norm_mix[i], mods[i], cache_k[:, slot], cache_v[:, slot], na)
            new_k.append(k_p)
            new_v.append(v_p)
        x = _moe_layer(x, norm_ffn[i], mods[i], i, moe_w_router[i], moe_b_router[i], moe_w_gu, moe_b_gu,
                       moe_w_down, moe_b_down)
    y_prompt = x[:N_PROMPT].reshape(BATCH, SEQ, D)
    y_sample = x[N_PROMPT:].reshape(DEC_BATCH, DEC_SEQ, D)
    return (y_prompt, y_sample, jnp.stack(new_wkv, axis=1), jnp.stack(new_k, axis=1), jnp.stack(new_v, axis=1))
```

```python
import functools

import jax
import jax.numpy as jnp
import numpy as np
from jax import lax
from jax.experimental import pallas as pl
from jax.experimental.pallas import tpu as pltpu

F32 = jnp.float32
BF16 = jnp.bfloat16

D = 1024
BATCH, SEQ = 16, 256
DEC_BATCH, DEC_SEQ = 4, 1024
DEPTH = 4
PAST_LEN = 512
GRID_W = 64
N_MIXERS = 3
N_MOD = 6
NORM_EPS = 1e-6
HEAD = 64
HEADS = D // HEAD
RW_GN_EPS = 64e-5
POOL_WINDOWS = (2, 4, 8, 16)
POOL_GROUP = D // len(POOL_WINDOWS)
NA_WIN_ROWS, NA_WIN_COLS = 8, 16
N_EXPERTS, TOP_K = 32, 4
D_EXPERT = D
SWIGLU_LIMIT, SWIGLU_ALPHA = 7.0, 1.702

N_PROMPT = BATCH * SEQ
N_LATENT = DEC_BATCH * DEC_SEQ
N_TOK = N_PROMPT + N_LATENT
COND_ROWS = 8
ROW_BLOCK = 1024
PROMPT_BLOCKS = N_PROMPT // ROW_BLOCK
LANES = 128
SUBLANES = 8
ROW_TILES = D // LANES
assert ROW_TILES == SUBLANES
HEADS_PER_STEP = LANES // HEAD
CHUNK = 64
WKV_GROUP = 4
WKV_PAIRS = 2
MOE_TILE = 256
MOE_ROWS = N_TOK * TOP_K + N_EXPERTS * MOE_TILE
MOE_TILES = MOE_ROWS // MOE_TILE
MOE_COL = 256
COMBINE_TILE = 256
NEG_BIG = -1e30
VMEM_LIMIT = 56 * 1024 * 1024


def _cond_row(block_1024):
    return jnp.maximum(block_1024 - (PROMPT_BLOCKS - 1), 0)


def _mods_spec(rows_per_block):
    per = ROW_BLOCK // rows_per_block
    return pl.BlockSpec((None, N_MOD, 1, D), lambda *ids: (_cond_row(ids[-1] // per), 0, 0, 0))


def _params(sem):
    return pltpu.CompilerParams(dimension_semantics=sem, vmem_limit_bytes=VMEM_LIMIT)


def _bdot(a, b):
    return jnp.dot(a.astype(BF16), b.astype(BF16), preferred_element_type=F32)


def _bdot_nt(a, b):
    return lax.dot_general(a.astype(BF16), b.astype(BF16), (((1,), (1,)), ((), ())),
                           preferred_element_type=F32)


def _bdot_tn(a, b):
    return lax.dot_general(a.astype(BF16), b.astype(BF16), (((0,), (0,)), ((), ())),
                           preferred_element_type=F32)


def _split3(x):
    hi = x.astype(BF16)
    r1 = x - hi.astype(F32)
    mid = r1.astype(BF16)
    lo = (r1 - mid.astype(F32)).astype(BF16)
    return hi, mid, lo


def _sigmoid(x):
    return 1.0 / (1.0 + jnp.exp(-x))


def _rms_mod(x, gain, shift, scale):
    y = x * lax.rsqrt(jnp.mean(x * x, axis=-1, keepdims=True) + NORM_EPS)
    return (y * gain) * (1.0 + scale) + shift


def _adaln_kernel(c_ref, w_ref, b_ref, o_ref):
    c = c_ref[...]
    s = c * _sigmoid(c)
    s_hi = s.astype(BF16)
    s_lo = (s - s_hi.astype(F32)).astype(BF16)
    w = w_ref[...]
    w_hi = w.astype(BF16)
    w_lo = (w - w_hi.astype(F32)).astype(BF16)
    o_ref[...] = (jnp.dot(s_hi, w_hi, preferred_element_type=F32) + jnp.dot(s_lo, w_hi, preferred_element_type=F32)
                  + jnp.dot(s_hi, w_lo, preferred_element_type=F32) + b_ref[...])


def _adaln(cond, w_mod, b_mod):
    out = pl.pallas_call(
        _adaln_kernel,
        out_shape=jax.ShapeDtypeStruct((DEPTH, COND_ROWS, N_MOD * D), F32),
        grid=(DEPTH, N_MOD),
        in_specs=[pl.BlockSpec((COND_ROWS, D), lambda l, j: (0, 0)),
                  pl.BlockSpec((None, D, D), lambda l, j: (l, 0, j)),
                  pl.BlockSpec((None, 1, D), lambda l, j: (l, 0, j))],
        out_specs=pl.BlockSpec((None, COND_ROWS, D), lambda l, j: (l, 0, j)),
        compiler_params=_params(("parallel", "parallel")),
        name="adaln",
    )(cond, w_mod, b_mod.reshape(DEPTH, 1, N_MOD * D))
    return out.reshape(DEPTH, COND_ROWS, N_MOD, 1, D)


def _normmod_kernel(x_ref, g_ref, m_ref, h_ref, *xx_ref, shift_idx, scale_idx):
    h = _rms_mod(x_ref[...], g_ref[...], m_ref[shift_idx], m_ref[scale_idx])
    h_ref[...] = h
    if xx_ref:
        seq = jnp.where(pl.program_id(0) < PROMPT_BLOCKS, SEQ, DEC_SEQ)
        t = lax.broadcasted_iota(jnp.int32, (ROW_BLOCK, 1), 0) & (seq - 1)
        prev = jnp.where(t == 0, 0.0, pltpu.roll(h, 1, 0))
        nxt = jnp.where(t == seq - 1, 0.0, pltpu.roll(h, ROW_BLOCK - 1, 0))
        xx_ref[0][...] = 0.5 * (prev + nxt) - h


def _normmod(x, gain, mods, with_xx):
    n_out = 2 if with_xx else 1
    row = pl.BlockSpec((ROW_BLOCK, D), lambda i: (i, 0))
    outs = pl.pallas_call(
        functools.partial(_normmod_kernel, shift_idx=0, scale_idx=1),
        out_shape=[jax.ShapeDtypeStruct((N_TOK, D), F32)] * n_out,
        grid=(N_TOK // ROW_BLOCK,),
        in_specs=[row, pl.BlockSpec((1, D), lambda i: (0, 0)), _mods_spec(ROW_BLOCK)],
        out_specs=[row] * n_out,
        compiler_params=_params(("parallel",)),
        name="normmod",
    )(x, gain.reshape(1, D), mods)
    return outs


PROJ_TILE = 512


def _proj3_kernel(*refs, mix):
    if mix:
        h_ref, xx_ref, mu_ref, w_ref, o_ref, wbf_ref = refs
    else:
        h_ref, w_ref, o_ref, wbf_ref = refs

    @pl.when(pl.program_id(1) == 0)
    def _():
        wbf_ref[...] = w_ref[...].astype(BF16)

    x = h_ref[...]
    if mix:
        x = x + xx_ref[...] * mu_ref[...]
    o_ref[...] = jnp.dot(x.astype(BF16), wbf_ref[...], preferred_element_type=F32)


def _proj3(h, w, xx=None, mu=None):
    mix = xx is not None
    row = pl.BlockSpec((PROJ_TILE, D), lambda j, i: (i, 0))
    if mix:
        ins = [h, xx, mu, w]
        specs = [row, row, pl.BlockSpec((None, 1, D), lambda j, i: (j, 0, 0)),
                 pl.BlockSpec((None, D, D), lambda j, i: (j, 0, 0))]
    else:
        ins = [h, w]
        specs = [row, pl.BlockSpec((D, D), lambda j, i: (0, j))]
    return pl.pallas_call(
        functools.partial(_proj3_kernel, mix=mix),
        out_shape=jax.ShapeDtypeStruct((3, N_TOK, D), F32),
        grid=(3, N_TOK // PROJ_TILE),
        in_specs=specs,
        out_specs=pl.BlockSpec((None, PROJ_TILE, D), lambda j, i: (j, i, 0)),
        scratch_shapes=[pltpu.VMEM((D, D), BF16)],
        compiler_params=_params(("arbitrary", "arbitrary")),
        name="proj3",
    )(*ins)


def _lora_kernel(h_ref, xx_ref, mu_ref, a_ref, b_ref, *rest, n, act, has_bias):
    if has_bias:
        bias_ref, o_ref = rest
    else:
        (o_ref,) = rest
    h = h_ref[...]
    xx = xx_ref[...]
    for j in range(n):
        x = h + xx * mu_ref[j]
        t = _bdot(x, a_ref[j])
        if act == "tanh":
            t = jnp.tanh(t)
        elif act == "sigmoid":
            t = _sigmoid(t)
        o = _bdot(t, b_ref[j])
        if has_bias:
            o = o + bias_ref[j]
        o_ref[j] = o


def _lora(h, xx, mu, a, b, bias, act):
    n, _, r = a.shape
    row = pl.BlockSpec((PROJ_TILE, D), lambda i: (i, 0))
    full = lambda shape: pl.BlockSpec(shape, lambda i: (0,) * len(shape))
    ins = [h, xx, mu, a, b]
    specs = [row, row, full((n, 1, D)), full((n, D, r)), full((n, r, D))]
    if bias is not None:
        ins.append(bias)
        specs.append(full((n, 1, D)))
    return pl.pallas_call(
        functools.partial(_lora_kernel, n=n, act=act, has_bias=bias is not None),
        out_shape=jax.ShapeDtypeStruct((n, N_TOK, D), F32),
        grid=(N_TOK // PROJ_TILE,),
        in_specs=specs,
        out_specs=pl.BlockSpec((n, PROJ_TILE, D), lambda i: (0, i, 0)),
        compiler_params=_params(("parallel",)),
        name="lora_" + act,
    )(*ins)


def _mmres_kernel(*refs, with_g, gate_idx):
    if with_g:
        z_ref, g_ref, w_ref, x_ref, m_ref, o_ref, wbf_ref = refs
    else:
        z_ref, w_ref, x_ref, m_ref, o_ref, wbf_ref = refs

    @pl.when(pl.program_id(0) == 0)
    def _():
        wbf_ref[...] = w_ref[...].astype(BF16)

    z = z_ref[...]
    if with_g:
        z = z * g_ref[...]
    o_ref[...] = x_ref[...] + m_ref[gate_idx] * jnp.dot(z.astype(BF16), wbf_ref[...],
                                                        preferred_element_type=F32)


def _mmres(z, w, x, mods, g=None):
    row = pl.BlockSpec((PROJ_TILE, D), lambda i: (i, 0))
    ins, specs = [z], [row]
    if g is not None:
        ins.append(g)
        specs.append(row)
    ins += [w, x, mods]
    specs += [pl.BlockSpec((D, D), lambda i: (0, 0)), row, _mods_spec(PROJ_TILE)]
    return pl.pallas_call(
        functools.partial(_mmres_kernel, with_g=g is not None, gate_idx=2),
        out_shape=jax.ShapeDtypeStruct((N_TOK, D), F32),
        grid=(N_TOK // PROJ_TILE,),
        in_specs=specs,
        out_specs=row,
        scratch_shapes=[pltpu.VMEM((D, D), BF16)],
        compiler_params=_params(("arbitrary",)),
        name="mmres",
    )(*ins)


def _block_diag(x, left):
    return jnp.concatenate([jnp.where(left, x, 0.0), jnp.where(left, 0.0, x)], axis=0)


def _wkvp_kernel(*refs, seq, has_s0, emit_state):
    it = iter(refs)
    rkv_ref, wl_ref, al_ref = next(it), next(it), next(it)
    s0_ref = next(it) if has_s0 else None
    kk_ref, ka_ref, rk_ref, lw_ref, lb_ref = next(it), next(it), next(it), next(it), next(it)
    z_ref = next(it)
    sf_ref = next(it) if emit_state else None
    w2_ref, of_ref, ry_ref, pc_ref, y_ref = next(it), next(it), next(it), next(it), next(it)
    n_chunks = seq // CHUNK
    P2 = 2 * HEAD

    ri = lax.broadcasted_iota(jnp.int32, (CHUNK, CHUNK), 0)
    ci = lax.broadcasted_iota(jnp.int32, (CHUNK, CHUNK), 1)
    tri_bf = ((ri >= ci).astype(BF16), (ri <= ci).astype(BF16))
    rp = lax.broadcasted_iota(jnp.int32, (CHUNK, P2), 0)
    lane = lax.broadcasted_iota(jnp.int32, (CHUNK, P2), 1)
    cp = lane & (HEAD - 1)
    left = lane < HEAD
    eye = (rp == cp).astype(F32)
    blk16 = (rp // 16) == (cp // 16)
    blk32 = (rp // 32) == (cp // 32)
    mask2 = (jnp.concatenate([rp > cp, rp >= cp], axis=0), jnp.concatenate([rp < cp, rp <= cp], axis=0))
    left2 = jnp.concatenate([left, left], axis=0)
    same_head = left2 == (lax.broadcasted_iota(jnp.int32, (2 * CHUNK, P2), 0) < HEAD)
    k_k, k_a, r_k = kk_ref[...], ka_ref[...], rk_ref[...]
    heads = tuple(slice(hh * HEAD, (hh + 1) * HEAD) for hh in range(WKV_PAIRS * 2))
    lanes = tuple(slice(q * P2, (q + 1) * P2) for q in range(WKV_PAIRS))
    bd = lambda x: _block_diag(x, left)

    def prepare(g, carry):
        ch = []
        for j in range(WKV_GROUP):
            cc = g * WKV_GROUP + j
            rows = pl.ds(pl.multiple_of(cc * CHUNK, CHUNK), CHUNK)
            r2, k2, v2 = rkv_ref[0, rows, :], rkv_ref[1, rows, :], rkv_ref[2, rows, :]
            kk2 = k2 * k_k
            kk2 = jnp.concatenate(
                [kk2[:, sl] / jnp.maximum(jnp.sqrt(jnp.sum(kk2[:, sl] * kk2[:, sl], axis=-1, keepdims=True)), 1e-12)
                 for sl in heads], axis=1)
            for d in range(2):
                w_in = -wl_ref[d, rows, :]
                softplus = jnp.maximum(w_in, 0.0) + jnp.log(1.0 + jnp.exp(-jnp.abs(w_in)))
                logdec = -jnp.exp(-softplus - 0.5)
                a2 = _sigmoid(al_ref[d, rows, :])
                hi, mid, lo = _split3(logdec)
                lp = (jnp.dot(tri_bf[d], hi, preferred_element_type=F32)
                      + jnp.dot(tri_bf[d], mid, preferred_element_type=F32)
                      + jnp.dot(tri_bf[d], lo, preferred_element_type=F32))
                lp_end = lp[CHUNK - 1:CHUNK, :] if d == 0 else lp[0:1, :]
                e_neg, p_end = jnp.exp(-lp), jnp.exp(lp_end)
                kt2 = kk2 * jnp.exp(lp - logdec)
                rt2 = r2 * jnp.exp(lp)
                ks2 = k2 * (1.0 + (a2 - 1.0) * k_a) * e_neg
                bs2 = kk2 * a2 * e_neg
                kh2, bh2 = ks2 * p_end, bs2 * p_end
                pc_ref[d, cc] = p_end
                for q, ql in enumerate(lanes):
                    ch.append(dict(cc=cc, rows=rows, d=d, q=q, ql=ql, kt=kt2[:, ql], rt=rt2[:, ql], ks=ks2[:, ql],
                                   bs=bs2[:, ql], kh=kh2[:, ql], bh=bh2[:, ql], v=v2[:, ql]))
        for c in ch:
            q2 = jnp.concatenate([c["kt"], c["rt"]], axis=0)
            c["a_k"] = jnp.where(mask2[c["d"]], _bdot_nt(q2, bd(c["ks"])), 0.0)
            c["a_b"] = jnp.where(mask2[c["d"]], _bdot_nt(q2, bd(c["bs"])), 0.0)
        for c in ch:
            c["av"] = _bdot(c["a_k"], bd(c["v"]))
        for c in ch:
            c["tri"] = c["a_b"][:CHUNK]
            d16 = jnp.where(blk16, c["tri"], 0.0)
            c["x"] = eye - d16
            c["p"] = _bdot(d16, bd(d16))
        for level in range(3):
            for c in ch:
                c["x"] = c["x"] + _bdot(c["x"], bd(c["p"]))
            if level < 2:
                for c in ch:
                    c["p"] = _bdot(c["p"], bd(c["p"]))
        for inner, outer in ((blk16, blk32), (blk32, None)):
            keep = (~inner) if outer is None else (outer & (~inner))
            for c in ch:
                c["t"] = _bdot(c["x"], bd(jnp.where(keep, c["tri"], 0.0)))
            for c in ch:
                c["x"] = c["x"] - _bdot(c["t"], bd(c["x"]))
        for c in ch:
            c["wu"] = _bdot(c["x"], jnp.concatenate([bd(c["kt"]), bd(c["av"][:CHUNK])], axis=1))
        for c in ch:
            wm, uv = c["wu"][:, :P2], c["wu"][:, P2:]
            corr = _bdot(c["a_b"][CHUNK:], jnp.concatenate([bd(wm), bd(uv)], axis=1))
            d, q, cc = c["d"], c["q"], c["cc"]
            ry_ref[d, q, cc] = (c["rt"] - corr[:, :P2]).astype(BF16)
            y_ref[d, c["rows"], c["ql"]] = c["av"][CHUNK:] - corr[:, P2:]
            w2_ref[d, q, cc] = jnp.where(same_head, _bdot_tn(wm, c["bh"]), 0.0).astype(BF16)
            full = _bdot_tn(jnp.concatenate([c["v"], -uv], axis=0), jnp.concatenate([c["kh"], c["bh"]], axis=0))
            of_ref[d, q, cc] = jnp.where(left, full[:HEAD], full[HEAD:])
        return carry

    lax.fori_loop(0, n_chunks // WKV_GROUP, prepare, 0)

    def advance(c, states):
        new = []
        for d in range(2):
            cc = c if d == 0 else n_chunks - 1 - c
            rows = pl.ds(pl.multiple_of(cc * CHUNK, CHUNK), CHUNK)
            p_end = pc_ref[d, cc]
            for q, ql in enumerate(lanes):
                s = states[d * WKV_PAIRS + q]
                y_ref[d, rows, ql] = y_ref[d, rows, ql] + _bdot_nt(ry_ref[d, q, cc], bd(s))
                new.append(s * p_end[:, ql] - _bdot(s, w2_ref[d, q, cc]) + of_ref[d, q, cc])
        return tuple(new)

    if has_s0:
        init = tuple(jnp.concatenate([s0_ref[d, 2 * q], s0_ref[d, 2 * q + 1]], axis=1)
                     for d in range(2) for q in range(WKV_PAIRS))
    else:
        init = tuple(jnp.zeros((HEAD, P2), F32) for _ in range(2 * WKV_PAIRS))
    final = lax.fori_loop(0, n_chunks, advance, init)

    if emit_state:
        for d in range(2):
            for q in range(WKV_PAIRS):
                s = final[d * WKV_PAIRS + q]
                sf_ref[d, 2 * q] = s[:, :HEAD]
                sf_ref[d, 2 * q + 1] = s[:, HEAD:]

    EP = 256
    lnx_w, lnx_b = lw_ref[...], lb_ref[...]

    def epilogue(i, carry):
        rows = pl.ds(pl.multiple_of(i * EP, EP), EP)
        r2, k2, v2 = rkv_ref[0, rows, :], rkv_ref[1, rows, :], rkv_ref[2, rows, :]
        y2 = y_ref[0, rows, :] + y_ref[1, rows, :]
        coef = 2.0 + (_sigmoid(al_ref[0, rows, :]) + _sigmoid(al_ref[1, rows, :]) - 2.0) * k_a
        rkr = r2 * k2 * coef * r_k
        yn, bonus = [], []
        for sl in heads:
            y = y2[:, sl]
            mean = jnp.mean(y, axis=-1, keepdims=True)
            var = jnp.mean(jnp.square(y - mean), axis=-1, keepdims=True)
            yn.append((y - mean) * lax.rsqrt(var + RW_GN_EPS))
            bonus.append(jnp.sum(rkr[:, sl], axis=-1, keepdims=True) * v2[:, sl])
        z_ref[rows, :] = jnp.concatenate(yn, axis=1) * lnx_w + lnx_b + jnp.concatenate(bonus, axis=1)
        return carry

    lax.fori_loop(0, seq // EP, epilogue, 0)


def _wkvp(rkv, wl, al, s0, k_k, k_a, r_k, lnx_w, lnx_b, *, batch, seq, row_block0, emit_state):
    has_s0 = s0 is not None
    width = WKV_PAIRS * 2 * HEAD
    steps = D // width
    n_chunks = seq // CHUNK
    tok = lambda lead: pl.BlockSpec((lead, seq, width), lambda b, p: (0, row_block0 + b, p))
    vec = pl.BlockSpec((1, width), lambda b, p: (0, p))
    st = pl.BlockSpec((None, 2, WKV_PAIRS * 2, HEAD, HEAD), lambda b, p: (b, 0, p, 0, 0))
    ins, specs = [rkv, wl, al], [tok(3), tok(2), tok(2)]
    if has_s0:
        ins.append(s0)
        specs.append(st)
    ins += [k_k.reshape(1, D), k_a.reshape(1, D), r_k.reshape(1, D), lnx_w.reshape(1, D), lnx_b.reshape(1, D)]
    specs += [vec] * 5
    out_shape = [jax.ShapeDtypeStruct((batch * seq, D), F32)]
    out_specs = [pl.BlockSpec((seq, width), lambda b, p: (b, p))]
    if emit_state:
        out_shape.append(jax.ShapeDtypeStruct((batch, 2, HEADS, HEAD, HEAD), F32))
        out_specs.append(st)
    return pl.pallas_call(
        functools.partial(_wkvp_kernel, seq=seq, has_s0=has_s0, emit_state=emit_state),
        out_shape=out_shape,
        grid=(batch, steps),
        in_specs=specs,
        out_specs=out_specs,
        scratch_shapes=[pltpu.VMEM((2, WKV_PAIRS, n_chunks, 2 * HEAD, 2 * HEAD), BF16),
                        pltpu.VMEM((2, WKV_PAIRS, n_chunks, HEAD, 2 * HEAD), F32),
                        pltpu.VMEM((2, WKV_PAIRS, n_chunks, CHUNK, 2 * HEAD), BF16),
                        pltpu.VMEM((2, n_chunks, 1, width), F32),
                        pltpu.VMEM((2, seq, width), F32)],
        compiler_params=_params(("parallel", "parallel")),
        name="wkv_%d" % seq,
    )(*ins)


def _pool_kernel(x_ref, g_ref, m_ref, w_ref, sc_ref, o_ref):
    x = x_ref[...]
    h = _rms_mod(x, g_ref[...], m_ref[0], m_ref[1])
    seq = jnp.where(pl.program_id(0) < PROMPT_BLOCKS, SEQ, DEC_SEQ)
    t = lax.broadcasted_iota(jnp.int32, (ROW_BLOCK, 1), 0) & (seq - 1)
    gate = m_ref[2]
    scale = sc_ref[...]
    for g, win in enumerate(POOL_WINDOWS):
        half = win // 2
        cols = slice(g * POOL_GROUP, (g + 1) * POOL_GROUP)
        hg = h[:, cols]
        up = lambda z, m: jnp.where(t + m <= seq - 1, pltpu.roll(z, ROW_BLOCK - m, 0), 0.0)
        down = lambda z, m: jnp.where(t - m >= 0, pltpu.roll(z, m, 0), 0.0)
        fwd = hg
        bwd = down(hg, 1)
        m = 1
        while m < half:
            fwd = fwd + up(fwd, m)
            bwd = bwd + down(bwd, m)
            m *= 2
        count = (jnp.minimum(t + half - 1, seq - 1) - jnp.maximum(t - half, 0) + 1).astype(F32)
        pooled = (fwd + bwd) / count - hg
        mixed = _bdot(pooled, w_ref[g]) * scale[:, cols]
        o_ref[:, cols] = x[:, cols] + gate[:, cols] * mixed


def _pool_layer(x, gain, mods, w_pool, scale):
    row = pl.BlockSpec((ROW_BLOCK, D), lambda i: (i, 0))
    n_g = len(POOL_WINDOWS)
    return pl.pallas_call(
        _pool_kernel,
        out_shape=jax.ShapeDtypeStruct((N_TOK, D), F32),
        grid=(N_TOK // ROW_BLOCK,),
        in_specs=[row, pl.BlockSpec((1, D), lambda i: (0, 0)), _mods_spec(ROW_BLOCK),
                  pl.BlockSpec((n_g, POOL_GROUP, POOL_GROUP), lambda i: (0, 0, 0)),
                  pl.BlockSpec((1, D), lambda i: (0, 0))],
        out_specs=row,
        compiler_params=_params(("parallel",)),
        name="pool",
    )(x, gain.reshape(1, D), mods, w_pool, scale.reshape(1, D))


NA_ROWS = DEC_SEQ // GRID_W
NA_WIN_R = min(NA_WIN_ROWS, NA_ROWS)


def _head_rms(x, gain):
    return x * lax.rsqrt(jnp.mean(x * x, axis=-1, keepdims=True) + NORM_EPS) * gain


def _attn_prompt_kernel(q_ref, k_ref, v_ref, qg_ref, kg_ref, o_ref, kn_ref):
    outs, kns = [], []
    for hh in range(HEADS_PER_STEP):
        sl = slice(hh * HEAD, (hh + 1) * HEAD)
        q = _head_rms(q_ref[:, sl], qg_ref[...]) * (HEAD ** -0.5)
        k = _head_rms(k_ref[:, sl], kg_ref[...])
        s = _bdot_nt(q, k)
        p = jnp.exp(s - jnp.max(s, axis=-1, keepdims=True))
        outs.append(_bdot(p, v_ref[:, sl]) / jnp.sum(p, axis=-1, keepdims=True))
        kns.append(k)
    o_ref[...] = jnp.concatenate(outs, axis=1)
    kn_ref[...] = jnp.concatenate(kns, axis=1)


def _attn_latent_kernel(q_ref, k_ref, v_ref, qg_ref, kg_ref, ck_ref, cv_ref, bias_ref, o_ref):
    outs = []
    for hh in range(HEADS_PER_STEP):
        sl = slice(hh * HEAD, (hh + 1) * HEAD)
        q = _head_rms(q_ref[:, sl], qg_ref[...]) * (HEAD ** -0.5)
        k = _head_rms(k_ref[:, sl], kg_ref[...])
        v = v_ref[:, sl]
        s_ctx = _bdot_nt(q, ck_ref[:, sl])
        m_ctx = jnp.max(s_ctx, axis=-1, keepdims=True)
        p_ctx = jnp.exp(s_ctx - m_ctx)
        l_ctx = jnp.sum(p_ctx, axis=-1, keepdims=True)
        o_ctx = _bdot(p_ctx, cv_ref[:, sl])
        rows_out = []
        for qr in range(NA_ROWS):
            r0 = min(max(qr - NA_WIN_R // 2, 0), NA_ROWS - NA_WIN_R)
            qs = slice(qr * GRID_W, (qr + 1) * GRID_W)
            ks = slice(r0 * GRID_W, (r0 + NA_WIN_R) * GRID_W)
            bias = jnp.concatenate([bias_ref[hh, r0 + j - qr + NA_WIN_ROWS - 1] for j in range(NA_WIN_R)], axis=1)
            s = _bdot_nt(q[qs], k[ks]) + bias
            m = jnp.maximum(jnp.max(s, axis=-1, keepdims=True), m_ctx[qs])
            p = jnp.exp(s - m)
            w_ctx = jnp.exp(m_ctx[qs] - m)
            den = jnp.sum(p, axis=-1, keepdims=True) + l_ctx[qs] * w_ctx
            rows_out.append((_bdot(p, v[ks]) + o_ctx[qs] * w_ctx) / den)
        outs.append(jnp.concatenate(rows_out, axis=0))
    o_ref[...] = jnp.concatenate(outs, axis=1)


def _attention_prompt(qkv, q_gain, k_gain):
    pair = HEADS // HEADS_PER_STEP
    tok = lambda j: pl.BlockSpec((None, SEQ, LANES), lambda b, p: (j, b, p))
    gain = pl.BlockSpec((1, HEAD), lambda b, p: (0, 0))
    out = pl.BlockSpec((SEQ, LANES), lambda b, p: (b, p))
    return pl.pallas_call(
        _attn_prompt_kernel,
        out_shape=[jax.ShapeDtypeStruct((N_PROMPT, D), F32)] * 2,
        grid=(BATCH, pair),
        in_specs=[tok(0), tok(1), tok(2), gain, gain],
        out_specs=[out, out],
        compiler_params=_params(("parallel", "parallel")),
        name="attn_prompt",
    )(qkv, qkv, qkv, q_gain.reshape(1, HEAD), k_gain.reshape(1, HEAD))


def _attention_latent(qkv, q_gain, k_gain, ck, cv, bias):
    pair = HEADS // HEADS_PER_STEP
    row0 = N_PROMPT // DEC_SEQ
    gain = pl.BlockSpec((1, HEAD), lambda p, b: (0, 0))
    tok = lambda j: pl.BlockSpec((None, DEC_SEQ, LANES), lambda p, b: (j, row0 + b, p))
    ctx = pl.BlockSpec((None, PAST_LEN, LANES), lambda p, b: (b, 0, p))
    n_dr = 2 * NA_WIN_ROWS - 1
    return pl.pallas_call(
        _attn_latent_kernel,
        out_shape=jax.ShapeDtypeStruct((N_LATENT, D), F32),
        grid=(pair, DEC_BATCH),
        in_specs=[tok(0), tok(1), tok(2), gain, gain, ctx, ctx,
                  pl.BlockSpec((HEADS_PER_STEP, n_dr, GRID_W, GRID_W), lambda p, b: (p, 0, 0, 0))],
        out_specs=pl.BlockSpec((DEC_SEQ, LANES), lambda p, b: (b, p)),
        compiler_params=_params(("parallel", "parallel")),
        name="attn_latent",
    )(qkv, qkv, qkv, q_gain.reshape(1, HEAD), k_gain.reshape(1, HEAD), ck, cv, bias)


def _column_bias_table(rpb):
    n_dr, n_dc = rpb.shape[1], rpb.shape[2]
    span = 2 * GRID_W - 1
    left = GRID_W - NA_WIN_COLS
    g = jnp.pad(rpb, ((0, 0), (0, 0), (left, span - n_dc - left)))
    flat = jnp.broadcast_to(g[:, :, None, :], (HEADS, n_dr, GRID_W, span)).reshape(HEADS, n_dr, GRID_W * span)
    table = flat[:, :, GRID_W - 1:GRID_W - 1 + GRID_W * (span - 1)].reshape(HEADS, n_dr, GRID_W, span - 1)
    table = table[..., :GRID_W]
    col = np.arange(GRID_W)
    c0 = np.clip(col - NA_WIN_COLS // 2, 0, GRID_W - NA_WIN_COLS)
    inside = (col[None, :] >= c0[:, None]) & (col[None, :] < c0[:, None] + NA_WIN_COLS)
    return jnp.where(inside[None, None], table, NEG_BIG)


ROUTER_TILE = 512


def _router_kernel(x_ref, g_ref, m_ref, w_ref, b_ref, h_ref, gate_ref):
    h = _rms_mod(x_ref[...], g_ref[...], m_ref[3], m_ref[4])
    _store_token_rows(h_ref, h)
    h_hi = h.astype(BF16)
    h_lo = (h - h_hi.astype(F32)).astype(BF16)
    w = w_ref[...]
    w_hi = w.astype(BF16)
    w_lo = (w - w_hi.astype(F32)).astype(BF16)
    logits = (jnp.dot(h_hi, w_hi, preferred_element_type=F32) + jnp.dot(h_hi, w_lo, preferred_element_type=F32)
              + jnp.dot(h_lo, w_hi, preferred_element_type=F32) + b_ref[...])
    lane = lax.broadcasted_iota(jnp.int32, logits.shape, 1)
    vals = logits
    top0 = None
    den = 0.0
    gates = jnp.full(logits.shape, -1.0, F32)
    for j in range(TOP_K):
        m = jnp.max(vals, axis=-1, keepdims=True)
        first = jnp.min(jnp.where(vals == m, lane, N_EXPERTS), axis=-1, keepdims=True)
        sel = lane == first
        if j == 0:
            top0 = m
        e = jnp.exp(m - top0)
        den = den + e
        gates = jnp.where(sel, e, gates)
        vals = jnp.where(sel, -jnp.inf, vals)
    gate_ref[...] = jnp.where(gates >= 0.0, gates / den, -1.0)


def _router(x, gain, mods, w_router, b_router):
    row = pl.BlockSpec((ROUTER_TILE, D), lambda i: (i, 0))
    return pl.pallas_call(
        _router_kernel,
        out_shape=[jax.ShapeDtypeStruct((N_TOK * ROW_TILES, LANES), F32),
                   jax.ShapeDtypeStruct((N_TOK, N_EXPERTS), F32)],
        grid=(N_TOK // ROUTER_TILE,),
        in_specs=[row, pl.BlockSpec((1, D), lambda i: (0, 0)), _mods_spec(ROUTER_TILE),
                  pl.BlockSpec((D, N_EXPERTS), lambda i: (0, 0)), pl.BlockSpec((1, N_EXPERTS), lambda i: (0, 0))],
        out_specs=[pl.BlockSpec((ROUTER_TILE * ROW_TILES, LANES), lambda i: (i, 0)),
                   pl.BlockSpec((ROUTER_TILE, N_EXPERTS), lambda i: (i, 0))],
        compiler_params=_params(("parallel",)),
        name="router",
    )(x, gain.reshape(1, D), mods, w_router, b_router.reshape(1, N_EXPERTS))


def _store_token_rows(ref, value, lead=()):
    n = value.shape[0]
    for c in range(ROW_TILES):
        ref[lead + (pl.ds(c, n, stride=ROW_TILES), slice(None))] = value[:, c * LANES:(c + 1) * LANES]


def _load_token_chunk(ref, n, c, lead=()):
    return ref[lead + (pl.ds(c, n, stride=ROW_TILES), slice(None))]


def _token_copy(src, src_tok, dst, dst_tok, sem):
    rows = lambda t: pl.ds(t * ROW_TILES if isinstance(t, int) else pl.multiple_of(t * ROW_TILES, ROW_TILES), ROW_TILES)
    return pltpu.make_async_copy(src.at[rows(src_tok)], dst.at[rows(dst_tok)], sem)


def _invert_kernel(pos_ref, dst_ref):
    def clear(i, carry):
        dst_ref[i] = 0
        return carry

    lax.fori_loop(0, MOE_ROWS, clear, 0, unroll=8)

    def place(n, carry):
        for j in range(TOP_K):
            dst_ref[pos_ref[n * TOP_K + j]] = j * N_TOK + n
        return carry

    lax.fori_loop(0, N_TOK, place, 0, unroll=2)


def _invert(pos4):
    return pl.pallas_call(
        _invert_kernel,
        out_shape=jax.ShapeDtypeStruct((MOE_ROWS,), jnp.int32),
        in_specs=[pl.BlockSpec(memory_space=pltpu.SMEM)],
        out_specs=pl.BlockSpec(memory_space=pltpu.SMEM),
        name="invert",
    )(pos4)


def _experts_kernel(te_ref, tv_ref, nt_ref, dst_ref, h_hbm, wgu_ref, bgu_ref, wd_ref, bd_ref, ys_hbm,
                    xbuf, ybuf, sem_in, sem_out, wgu_bf, wd_bf, act_ref):
    t = pl.program_id(0)
    n_tiles = nt_ref[0]
    slot = t % 2
    other = 1 - slot

    def gather_row(tile, r, to_slot):
        tok = dst_ref[tile * MOE_TILE + r] & (N_TOK - 1)
        return _token_copy(h_hbm, tok, xbuf.at[to_slot], r, sem_in.at[to_slot])

    def scatter_row(tile, r, from_slot):
        return _token_copy(ybuf.at[from_slot], r, ys_hbm, dst_ref[tile * MOE_TILE + r], sem_out.at[from_slot])

    def gather_wait(s):
        pltpu.make_async_copy(h_hbm.at[pl.ds(0, MOE_TILE * ROW_TILES)], xbuf.at[s], sem_in.at[s]).wait()

    def scatter_wait(s, tokens):
        @pl.when(tokens > 0)
        def _():
            rows = pl.ds(0, pl.multiple_of(tokens * ROW_TILES, ROW_TILES))
            pltpu.make_async_copy(ybuf.at[s].at[rows], ys_hbm.at[rows], sem_out.at[s]).wait()

    @pl.when(t == 0)
    def _():
        def issue(r, carry):
            gather_row(0, r, 0).start()
            return carry

        lax.fori_loop(0, MOE_TILE, issue, 0)

    @pl.when((t >= 2) & (t < n_tiles))
    def _():
        scatter_wait(slot, tv_ref[t - 1])

    @pl.when(t < n_tiles)
    def _():
        @pl.when((t == 0) | (te_ref[t] != te_ref[jnp.maximum(t - 1, 0)]))
        def _():
            wgu_bf[...] = wgu_ref[...].astype(BF16)
            wd_bf[...] = wd_ref[...].astype(BF16)

        gather_wait(slot)
        nxt = jnp.minimum(t + 1, n_tiles - 1)
        prev, prev_rows = jnp.maximum(t - 1, 0), tv_ref[t]
        n_up, n_down = D_EXPERT // MOE_COL, D // MOE_COL

        def issue_gathers(block):
            for r in range(block * (MOE_TILE // n_up), (block + 1) * (MOE_TILE // n_up)):
                gather_row(nxt, r, other).start()

        def issue_scatters(block):
            for r in range(block * (MOE_TILE // n_down), (block + 1) * (MOE_TILE // n_down)):
                @pl.when(r < prev_rows)
                def _():
                    scatter_row(prev, r, other).start()

        x = jnp.concatenate([_load_token_chunk(xbuf, MOE_TILE, c, (slot,)) for c in range(ROW_TILES)],
                            axis=1).astype(BF16)
        for c in range(n_up):
            cols = slice(c * MOE_COL, (c + 1) * MOE_COL)
            ups = slice(D_EXPERT + c * MOE_COL, D_EXPERT + (c + 1) * MOE_COL)
            glu = jnp.dot(x, wgu_bf[:, cols], preferred_element_type=F32) + bgu_ref[:, cols]
            lin = jnp.dot(x, wgu_bf[:, ups], preferred_element_type=F32) + bgu_ref[:, ups]
            glu = jnp.minimum(glu, SWIGLU_LIMIT)
            lin = jnp.clip(lin, -SWIGLU_LIMIT, SWIGLU_LIMIT)
            act_ref[:, cols] = (glu * _sigmoid(SWIGLU_ALPHA * glu) * (lin + 1.0)).astype(BF16)
            issue_gathers(c)
        act = act_ref[...]
        for c in range(n_down):
            cols = slice(c * MOE_COL, (c + 1) * MOE_COL)
            y = jnp.dot(act, wd_bf[:, cols], preferred_element_type=F32) + bd_ref[:, cols]
            for k in range(MOE_COL // LANES):
                chunk = c * (MOE_COL // LANES) + k
                ybuf[slot, pl.ds(chunk, MOE_TILE, stride=ROW_TILES), :] = y[:, k * LANES:(k + 1) * LANES]
            issue_scatters(c)

    @pl.when(t == n_tiles - 1)
    def _():
        def issue(r, carry):
            scatter_row(t, r, slot).start()
            return carry

        lax.fori_loop(0, tv_ref[t + 1], issue, 0)
        gather_wait(other)
        scatter_wait(other, tv_ref[t])
        scatter_wait(slot, tv_ref[t + 1])


def _experts(h, tile_expert, tile_rows, n_tiles, dst, layer, w_gu, b_gu, w_down, b_down):
    weight = lambda shape: pl.BlockSpec((None, None) + shape, lambda t, te, tv, nt, dst: (layer, te[t], 0, 0))
    grid_spec = pltpu.PrefetchScalarGridSpec(
        num_scalar_prefetch=4,
        grid=(MOE_TILES,),
        in_specs=[pl.BlockSpec(memory_space=pl.ANY),
                  weight((D, 2 * D_EXPERT)), weight((1, 2 * D_EXPERT)), weight((D_EXPERT, D)), weight((1, D))],
        out_specs=pl.BlockSpec(memory_space=pl.ANY),
        scratch_shapes=[pltpu.VMEM((2, MOE_TILE * ROW_TILES, LANES), F32),
                        pltpu.VMEM((2, MOE_TILE * ROW_TILES, LANES), F32),
                        pltpu.SemaphoreType.DMA((2,)), pltpu.SemaphoreType.DMA((2,)),
                        pltpu.VMEM((D, 2 * D_EXPERT), BF16), pltpu.VMEM((D_EXPERT, D), BF16),
                        pltpu.VMEM((MOE_TILE, D_EXPERT), BF16)],
    )
    return pl.pallas_call(
        _experts_kernel,
        out_shape=jax.ShapeDtypeStruct((TOP_K * N_TOK * ROW_TILES, LANES), F32),
        grid_spec=grid_spec,
        compiler_params=_params(("arbitrary",)),
        name="experts",
    )(tile_expert, tile_rows, n_tiles, dst, h, w_gu, b_gu.reshape(DEPTH, N_EXPERTS, 1, 2 * D_EXPERT),
      w_down, b_down.reshape(DEPTH, N_EXPERTS, 1, D))


def _combine_kernel(y0_ref, y1_ref, y2_ref, y3_ref, gate_ref, x_ref, m_ref, o_ref):
    gate = gate_ref[...]
    g = [gate[:, j:j + 1] for j in range(TOP_K)]
    scale = m_ref[5]
    for c in range(ROW_TILES):
        cols = slice(c * LANES, (c + 1) * LANES)
        y = [_load_token_chunk(ref, COMBINE_TILE, c) for ref in (y0_ref, y1_ref, y2_ref, y3_ref)]
        moe = (y[0] * g[0] + y[1] * g[1]) + (y[2] * g[2] + y[3] * g[3])
        o_ref[:, cols] = x_ref[:, cols] + scale[:, cols] * moe


def _combine(ys, gate4, x, mods):
    row = pl.BlockSpec((COMBINE_TILE, D), lambda i: (i, 0))
    blocks = N_TOK // COMBINE_TILE
    choice = lambda j: pl.BlockSpec((COMBINE_TILE * ROW_TILES, LANES), lambda i: (j * blocks + i, 0))
    return pl.pallas_call(
        _combine_kernel,
        out_shape=jax.ShapeDtypeStruct((N_TOK, D), F32),
        grid=(blocks,),
        in_specs=[choice(j) for j in range(TOP_K)]
        + [pl.BlockSpec((COMBINE_TILE, LANES), lambda i: (i, 0)), row, _mods_spec(COMBINE_TILE)],
        out_specs=row,
        compiler_params=_params(("parallel",)),
        name="combine",
    )(ys, ys, ys, ys, gate4, x, mods)


def _moe_layer(x, gain, mods, layer, w_router, b_router, w_gu, b_gu, w_down, b_down):
    h, gates = _router(x, gain, mods, w_router, b_router)
    sel = gates >= 0.0
    sel_i = sel.astype(jnp.int32)
    rank = jnp.cumsum(sel_i, axis=0) - sel_i
    count = jnp.sum(sel_i, axis=0)
    padded = ((count + MOE_TILE - 1) // MOE_TILE) * MOE_TILE
    group_end = jnp.cumsum(padded)
    pos = group_end[None, :] - padded[None, :] + rank
    slot = jnp.cumsum(sel_i, axis=1) - 1
    pick = [sel & (slot == j) for j in range(TOP_K)]
    pos4 = jnp.stack([jnp.sum(jnp.where(m, pos, 0), axis=1) for m in pick], axis=1).astype(jnp.int32)
    gate4 = jnp.stack([jnp.sum(jnp.where(m, gates, 0.0), axis=1) for m in pick], axis=1)
    gate4 = jnp.pad(gate4, ((0, 0), (0, LANES - TOP_K)))
    pos4 = pos4.reshape(-1)
    n_tiles = (group_end[-1] // MOE_TILE).astype(jnp.int32)
    tile_start = jnp.minimum(jnp.arange(MOE_TILES, dtype=jnp.int32) * MOE_TILE, group_end[-1] - 1)
    tile_expert = jnp.sum((group_end[None, :] <= tile_start[:, None]).astype(jnp.int32), axis=1)
    tile_expert = jnp.minimum(tile_expert, N_EXPERTS - 1).astype(jnp.int32)
    real_end = group_end - padded + count
    tile_rows = jnp.clip(real_end[tile_expert] - jnp.arange(MOE_TILES, dtype=jnp.int32) * MOE_TILE, 0, MOE_TILE)
    tile_rows = jnp.concatenate([jnp.zeros((1,), jnp.int32), tile_rows.astype(jnp.int32)])
    ys = _experts(h, tile_expert, tile_rows, n_tiles.reshape(1), _invert(pos4), layer, w_gu, b_gu, w_down, b_down)
    return _combine(ys, gate4, x, mods)


def _rwkv_layer(x, gain, mods, state, p):
    (mu, w_rkv, w_out, w0, w1, w2, a0, a1, a2, g1, g2, k_k, k_a, r_k, lnx_w, lnx_b) = p
    h, xx = _normmod(x, gain, mods, with_xx=True)
    mu = mu.reshape(N_MOD, 1, D)
    rkv = _proj3(h, w_rkv, xx, mu[jnp.array([0, 2, 3])])
    wl = _lora(h, xx, mu[jnp.array([1, 1])], w1, w2, w0.reshape(2, 1, D), "tanh")
    al = _lora(h, xx, mu[jnp.array([4, 4])], a1, a2, a0.reshape(2, 1, D), "none")
    g = _lora(h, xx, mu[5:6], g1[None], g2[None], None, "sigmoid")[0]
    vecs = (k_k, k_a, r_k, lnx_w, lnx_b)
    z_p, s_ctx = _wkvp(rkv, wl, al, None, *vecs, batch=BATCH, seq=SEQ, row_block0=0, emit_state=True)
    (z_s,) = _wkvp(rkv, wl, al, state, *vecs, batch=DEC_BATCH, seq=DEC_SEQ,
                  row_block0=N_PROMPT // DEC_SEQ, emit_state=False)
    z = jnp.concatenate([z_p, z_s], axis=0)
    return _mmres(z, w_out, x, mods, g=g), s_ctx


def _na_layer(x, gain, mods, ck, cv, p):
    w_qkv, w_out, q_norm, k_norm, rpb = p
    (h,) = _normmod(x, gain, mods, with_xx=False)
    qkv = _proj3(h, w_qkv)
    o_p, k_p = _attention_prompt(qkv, q_norm, k_norm)
    o_s = _attention_latent(qkv, q_norm, k_norm, ck.reshape(DEC_BATCH, PAST_LEN, D),
                            cv.reshape(DEC_BATCH, PAST_LEN, D), _column_bias_table(rpb))
    o = jnp.concatenate([o_p, o_s], axis=0)
    new_k = k_p.reshape(BATCH, SEQ, HEADS, HEAD)
    new_v = qkv[2, :N_PROMPT].reshape(BATCH, SEQ, HEADS, HEAD)
    return _mmres(o, w_out, x, mods), new_k, new_v


def kernel(x_prompt, x_sample, c, c_ctx, state_wkv, cache_k, cache_v, norm_mix, norm_ffn, w_mod, b_mod, rw_mu, rw_w_rkv, rw_w_out, rw_w0, rw_w1, rw_w2, rw_a0, rw_a1, rw_a2, rw_g1, rw_g2, rw_k_k, rw_k_a, rw_r_k, rw_lnx_w, rw_lnx_b, pool_w, pool_scale, na_w_qkv, na_w_out, na_q_norm, na_k_norm, na_rpb, moe_w_router, moe_b_router, moe_w_gu, moe_b_gu, moe_w_down, moe_b_down):
    x = jnp.concatenate([x_prompt.reshape(N_PROMPT, D), x_sample.reshape(N_LATENT, D)], axis=0)
    cond = jnp.concatenate([c_ctx[None, :], c, jnp.zeros((COND_ROWS - 1 - DEC_BATCH, D), F32)], axis=0)
    mods = _adaln(cond, w_mod, b_mod)
    new_wkv, new_k, new_v = [], [], []
    for i in range(DEPTH):
        kind, slot = i % N_MIXERS, i // N_MIXERS
        if kind == 0:
            rw = (rw_mu[slot], rw_w_rkv[slot], rw_w_out[slot], rw_w0[slot], rw_w1[slot], rw_w2[slot],
                  rw_a0[slot], rw_a1[slot], rw_a2[slot], rw_g1[slot], rw_g2[slot], rw_k_k[slot],
                  rw_k_a[slot], rw_r_k[slot], rw_lnx_w[slot], rw_lnx_b[slot])
            x, s_ctx = _rwkv_layer(x, norm_mix[i], mods[i], state_wkv[:, slot], rw)
            new_wkv.append(s_ctx)
        elif kind == 1:
            x = _pool_layer(x, norm_mix[i], mods[i], pool_w[slot], pool_scale[slot])
        else:
            na = (na_w_qkv[slot], na_w_out[slot], na_q_norm[slot], na_k_norm[slot], na_rpb[slot])
            x, k_p, v_p = _na_layer(x, norm_mix[i], mods[i], cache_k[:, slot], cache_v[:, slot], na)
            new_k.append(k_p)
            new_v.append(v_p)
        x = _moe_layer(x, norm_ffn[i], mods[i], i, moe_w_router[i], moe_b_router[i], moe_w_gu, moe_b_gu,
                       moe_w_down, moe_b_down)
    y_prompt = x[:N_PROMPT].reshape(BATCH, SEQ, D)
    y_sample = x[N_PROMPT:].reshape(DEC_BATCH, DEC_SEQ, D)
    return (y_prompt, y_sample, jnp.stack(new_wkv, axis=1), jnp.stack(new_k, axis=1), jnp.stack(new_v, axis=1))
```

```python
import functools

import jax
import jax.numpy as jnp
import numpy as np
from jax import lax
from jax.experimental import pallas as pl
from jax.experimental.pallas import tpu as pltpu

F32 = jnp.float32
BF16 = jnp.bfloat16

D = 1024
BATCH, SEQ = 16, 256
DEC_BATCH, DEC_SEQ = 4, 1024
DEPTH = 4
PAST_LEN = 512
GRID_W = 64
N_MIXERS = 3
N_MOD = 6
NORM_EPS = 1e-6
HEAD = 64
HEADS = D // HEAD
RW_GN_EPS = 64e-5
POOL_WINDOWS = (2, 4, 8, 16)
POOL_GROUP = D // len(POOL_WINDOWS)
NA_WIN_ROWS, NA_WIN_COLS = 8, 16
N_EXPERTS, TOP_K = 32, 4
D_EXPERT = D
SWIGLU_LIMIT, SWIGLU_ALPHA = 7.0, 1.702

N_PROMPT = BATCH * SEQ
N_LATENT = DEC_BATCH * DEC_SEQ
N_TOK = N_PROMPT + N_LATENT
COND_ROWS = 8
ROW_BLOCK = 1024
PROMPT_BLOCKS = N_PROMPT // ROW_BLOCK
LANES = 128
SUBLANES = 8
ROW_TILES = D // LANES
assert ROW_TILES == SUBLANES
HEADS_PER_STEP = LANES // HEAD
CHUNK = 64
WKV_GROUP = 4
WKV_PAIRS = 2
MOE_TILE = 256
MOE_ROWS = N_TOK * TOP_K + N_EXPERTS * MOE_TILE
MOE_TILES = MOE_ROWS // MOE_TILE
MOE_COL = 256
COMBINE_TILE = 256
NEG_BIG = -1e30
VMEM_LIMIT = 56 * 1024 * 1024


def _cond_row(block_1024):
    return jnp.maximum(block_1024 - (PROMPT_BLOCKS - 1), 0)


def _mods_spec(rows_per_block):
    per = ROW_BLOCK // rows_per_block
    return pl.BlockSpec((None, N_MOD, 1, D), lambda *ids: (_cond_row(ids[-1] // per), 0, 0, 0))


def _params(sem):
    return pltpu.CompilerParams(dimension_semantics=sem, vmem_limit_bytes=VMEM_LIMIT)


def _bdot(a, b):
    return jnp.dot(a.astype(BF16), b.astype(BF16), preferred_element_type=F32)


def _bdot_nt(a, b):
    return lax.dot_general(a.astype(BF16), b.astype(BF16), (((1,), (1,)), ((), ())),
                           preferred_element_type=F32)


def _bdot_tn(a, b):
    return lax.dot_general(a.astype(BF16), b.astype(BF16), (((0,), (0,)), ((), ())),
                           preferred_element_type=F32)


def _split3(x):
    hi = x.astype(BF16)
    r1 = x - hi.astype(F32)
    mid = r1.astype(BF16)
    lo = (r1 - mid.astype(F32)).astype(BF16)
    return hi, mid, lo


def _sigmoid(x):
    return 1.0 / (1.0 + jnp.exp(-x))


def _rms_mod(x, gain, shift, scale):
    y = x * lax.rsqrt(jnp.mean(x * x, axis=-1, keepdims=True) + NORM_EPS)
    return (y * gain) * (1.0 + scale) + shift


def _adaln_kernel(c_ref, w_ref, b_ref, o_ref):
    c = c_ref[...]
    s = c * _sigmoid(c)
    s_hi = s.astype(BF16)
    s_lo = (s - s_hi.astype(F32)).astype(BF16)
    w = w_ref[...]
    w_hi = w.astype(BF16)
    w_lo = (w - w_hi.astype(F32)).astype(BF16)
    o_ref[...] = (jnp.dot(s_hi, w_hi, preferred_element_type=F32) + jnp.dot(s_lo, w_hi, preferred_element_type=F32)
                  + jnp.dot(s_hi, w_lo, preferred_element_type=F32) + b_ref[...])


def _adaln(cond, w_mod, b_mod):
    out = pl.pallas_call(
        _adaln_kernel,
        out_shape=jax.ShapeDtypeStruct((DEPTH, COND_ROWS, N_MOD * D), F32),
        grid=(DEPTH, N_MOD),
        in_specs=[pl.BlockSpec((COND_ROWS, D), lambda l, j: (0, 0)),
                  pl.BlockSpec((None, D, D), lambda l, j: (l, 0, j)),
                  pl.BlockSpec((None, 1, D), lambda l, j: (l, 0, j))],
        out_specs=pl.BlockSpec((None, COND_ROWS, D), lambda l, j: (l, 0, j)),
        compiler_params=_params(("parallel", "parallel")),
        name="adaln",
    )(cond, w_mod, b_mod.reshape(DEPTH, 1, N_MOD * D))
    return out.reshape(DEPTH, COND_ROWS, N_MOD, 1, D)


def _normmod_kernel(x_ref, g_ref, m_ref, h_ref, *xx_ref, shift_idx, scale_idx):
    h = _rms_mod(x_ref[...], g_ref[...], m_ref[shift_idx], m_ref[scale_idx])
    h_ref[...] = h
    if xx_ref:
        seq = jnp.where(pl.program_id(0) < PROMPT_BLOCKS, SEQ, DEC_SEQ)
        t = lax.broadcasted_iota(jnp.int32, (ROW_BLOCK, 1), 0) & (seq - 1)
        prev = jnp.where(t == 0, 0.0, pltpu.roll(h, 1, 0))
        nxt = jnp.where(t == seq - 1, 0.0, pltpu.roll(h, ROW_BLOCK - 1, 0))
        xx_ref[0][...] = 0.5 * (prev + nxt) - h


def _normmod(x, gain, mods, with_xx):
    n_out = 2 if with_xx else 1
    row = pl.BlockSpec((ROW_BLOCK, D), lambda i: (i, 0))
    outs = pl.pallas_call(
        functools.partial(_normmod_kernel, shift_idx=0, scale_idx=1),
        out_shape=[jax.ShapeDtypeStruct((N_TOK, D), F32)] * n_out,
        grid=(N_TOK // ROW_BLOCK,),
        in_specs=[row, pl.BlockSpec((1, D), lambda i: (0, 0)), _mods_spec(ROW_BLOCK)],
        out_specs=[row] * n_out,
        compiler_params=_params(("parallel",)),
        name="normmod",
    )(x, gain.reshape(1, D), mods)
    return outs


PROJ_TILE = 512


def _proj3_kernel(*refs, mix):
    if mix:
        h_ref, xx_ref, mu_ref, w_ref, o_ref, wbf_ref = refs
    else:
        h_ref, w_ref, o_ref, wbf_ref = refs

    @pl.when(pl.program_id(1) == 0)
    def _():
        wbf_ref[...] = w_ref[...].astype(BF16)

    x = h_ref[...]
    if mix:
        x = x + xx_ref[...] * mu_ref[...]
    o_ref[...] = jnp.dot(x.astype(BF16), wbf_ref[...], preferred_element_type=F32)


def _proj3(h, w, xx=None, mu=None):
    mix = xx is not None
    row = pl.BlockSpec((PROJ_TILE, D), lambda j, i: (i, 0))
    if mix:
        ins = [h, xx, mu, w]
        specs = [row, row, pl.BlockSpec((None, 1, D), lambda j, i: (j, 0, 0)),
                 pl.BlockSpec((None, D, D), lambda j, i: (j, 0, 0))]
    else:
        ins = [h, w]
        specs = [row, pl.BlockSpec((D, D), lambda j, i: (0, j))]
    return pl.pallas_call(
        functools.partial(_proj3_kernel, mix=mix),
        out_shape=jax.ShapeDtypeStruct((3, N_TOK, D), F32),
        grid=(3, N_TOK // PROJ_TILE),
        in_specs=specs,
        out_specs=pl.BlockSpec((None, PROJ_TILE, D), lambda j, i: (j, i, 0)),
        scratch_shapes=[pltpu.VMEM((D, D), BF16)],
        compiler_params=_params(("arbitrary", "arbitrary")),
        name="proj3",
    )(*ins)


def _lora_kernel(h_ref, xx_ref, mu_ref, a_ref, b_ref, *rest, n, act, has_bias):
    if has_bias:
        bias_ref, o_ref = rest
    else:
        (o_ref,) = rest
    h = h_ref[...]
    xx = xx_ref[...]
    for j in range(n):
        x = h + xx * mu_ref[j]
        t = _bdot(x, a_ref[j])
        if act == "tanh":
            t = jnp.tanh(t)
        elif act == "sigmoid":
            t = _sigmoid(t)
        o = _bdot(t, b_ref[j])
        if has_bias:
            o = o + bias_ref[j]
        o_ref[j] = o


def _lora(h, xx, mu, a, b, bias, act):
    n, _, r = a.shape
    row = pl.BlockSpec((PROJ_TILE, D), lambda i: (i, 0))
    full = lambda shape: pl.BlockSpec(shape, lambda i: (0,) * len(shape))
    ins = [h, xx, mu, a, b]
    specs = [row, row, full((n, 1, D)), full((n, D, r)), full((n, r, D))]
    if bias is not None:
        ins.append(bias)
        specs.append(full((n, 1, D)))
    return pl.pallas_call(
        functools.partial(_lora_kernel, n=n, act=act, has_bias=bias is not None),
        out_shape=jax.ShapeDtypeStruct((n, N_TOK, D), F32),
        grid=(N_TOK // PROJ_TILE,),
        in_specs=specs,
        out_specs=pl.BlockSpec((n, PROJ_TILE, D), lambda i: (0, i, 0)),
        compiler_params=_params(("parallel",)),
        name="lora_" + act,
    )(*ins)


def _mmres_kernel(*refs, with_g, gate_idx):
    if with_g:
        z_ref, g_ref, w_ref, x_ref, m_ref, o_ref, wbf_ref = refs
    else:
        z_ref, w_ref, x_ref, m_ref, o_ref, wbf_ref = refs

    @pl.when(pl.program_id(0) == 0)
    def _():
        wbf_ref[...] = w_ref[...].astype(BF16)

    z = z_ref[...]
    if with_g:
        z = z * g_ref[...]
    o_ref[...] = x_ref[...] + m_ref[gate_idx] * jnp.dot(z.astype(BF16), wbf_ref[...],
                                                        preferred_element_type=F32)


def _mmres(z, w, x, mods, g=None):
    row = pl.BlockSpec((PROJ_TILE, D), lambda i: (i, 0))
    ins, specs = [z], [row]
    if g is not None:
        ins.append(g)
        specs.append(row)
    ins += [w, x, mods]
    specs += [pl.BlockSpec((D, D), lambda i: (0, 0)), row, _mods_spec(PROJ_TILE)]
    return pl.pallas_call(
        functools.partial(_mmres_kernel, with_g=g is not None, gate_idx=2),
        out_shape=jax.ShapeDtypeStruct((N_TOK, D), F32),
        grid=(N_TOK // PROJ_TILE,),
        in_specs=specs,
        out_specs=row,
        scratch_shapes=[pltpu.VMEM((D, D), BF16)],
        compiler_params=_params(("arbitrary",)),
        name="mmres",
    )(*ins)


def _block_diag(x, left):
    return jnp.concatenate([jnp.where(left, x, 0.0), jnp.where(left, 0.0, x)], axis=0)


def _wkvp_kernel(*refs, seq, has_s0, emit_state):
    it = iter(refs)
    rkv_ref, wl_ref, al_ref = next(it), next(it), next(it)
    s0_ref = next(it) if has_s0 else None
    kk_ref, ka_ref, rk_ref, lw_ref, lb_ref = next(it), next(it), next(it), next(it), next(it)
    z_ref = next(it)
    sf_ref = next(it) if emit_state else None
    w2_ref, of_ref, ry_ref, pc_ref, y_ref = next(it), next(it), next(it), next(it), next(it)
    n_chunks = seq // CHUNK
    P2 = 2 * HEAD

    ri = lax.broadcasted_iota(jnp.int32, (CHUNK, CHUNK), 0)
    ci = lax.broadcasted_iota(jnp.int32, (CHUNK, CHUNK), 1)
    tri_bf = ((ri >= ci).astype(BF16), (ri <= ci).astype(BF16))
    rp = lax.broadcasted_iota(jnp.int32, (CHUNK, P2), 0)
    lane = lax.broadcasted_iota(jnp.int32, (CHUNK, P2), 1)
    cp = lane & (HEAD - 1)
    left = lane < HEAD
    eye = (rp == cp).astype(F32)
    blk16 = (rp // 16) == (cp // 16)
    blk32 = (rp // 32) == (cp // 32)
    mask2 = (jnp.concatenate([rp > cp, rp >= cp], axis=0), jnp.concatenate([rp < cp, rp <= cp], axis=0))
    left2 = jnp.concatenate([left, left], axis=0)
    same_head = left2 == (lax.broadcasted_iota(jnp.int32, (2 * CHUNK, P2), 0) < HEAD)
    k_k, k_a, r_k = kk_ref[...], ka_ref[...], rk_ref[...]
    heads = tuple(slice(hh * HEAD, (hh + 1) * HEAD) for hh in range(WKV_PAIRS * 2))
    lanes = tuple(slice(q * P2, (q + 1) * P2) for q in range(WKV_PAIRS))
    bd = lambda x: _block_diag(x, left)

    def prepare(g, carry):
        ch = []
        for j in range(WKV_GROUP):
            cc = g * WKV_GROUP + j
            rows = pl.ds(pl.multiple_of(cc * CHUNK, CHUNK), CHUNK)
            r2, k2, v2 = rkv_ref[0, rows, :], rkv_ref[1, rows, :], rkv_ref[2, rows, :]
            kk2 = k2 * k_k
            kk2 = jnp.concatenate(
                [kk2[:, sl] / jnp.maximum(jnp.sqrt(jnp.sum(kk2[:, sl] * kk2[:, sl], axis=-1, keepdims=True)), 1e-12)
                 for sl in heads], axis=1)
            for d in range(2):
                w_in = -wl_ref[d, rows, :]
                softplus = jnp.maximum(w_in, 0.0) + jnp.log(1.0 + jnp.exp(-jnp.abs(w_in)))
                logdec = -jnp.exp(-softplus - 0.5)
                a2 = _sigmoid(al_ref[d, rows, :])
                hi, mid, lo = _split3(logdec)
                lp = (jnp.dot(tri_bf[d], hi, preferred_element_type=F32)
                      + jnp.dot(tri_bf[d], mid, preferred_element_type=F32)
                      + jnp.dot(tri_bf[d], lo, preferred_element_type=F32))
                lp_end = lp[CHUNK - 1:CHUNK, :] if d == 0 else lp[0:1, :]
                e_neg, p_end = jnp.exp(-lp), jnp.exp(lp_end)
                kt2 = kk2 * jnp.exp(lp - logdec)
                rt2 = r2 * jnp.exp(lp)
                ks2 = k2 * (1.0 + (a2 - 1.0) * k_a) * e_neg
                bs2 = kk2 * a2 * e_neg
                kh2, bh2 = ks2 * p_end, bs2 * p_end
                pc_ref[d, cc] = p_end
                for q, ql in enumerate(lanes):
                    ch.append(dict(cc=cc, rows=rows, d=d, q=q, ql=ql, kt=kt2[:, ql], rt=rt2[:, ql], ks=ks2[:, ql],
                                   bs=bs2[:, ql], kh=kh2[:, ql], bh=bh2[:, ql], v=v2[:, ql]))
        for c in ch:
            q2 = jnp.concatenate([c["kt"], c["rt"]], axis=0)
            c["a_k"] = jnp.where(mask2[c["d"]], _bdot_nt(q2, bd(c["ks"])), 0.0)
            c["a_b"] = jnp.where(mask2[c["d"]], _bdot_nt(q2, bd(c["bs"])), 0.0)
        for c in ch:
            c["av"] = _bdot(c["a_k"], bd(c["v"]))
        for c in ch:
            c["tri"] = c["a_b"][:CHUNK]
            d16 = jnp.where(blk16, c["tri"], 0.0)
            c["x"] = eye - d16
            c["p"] = _bdot(d16, bd(d16))
        for level in range(3):
            for c in ch:
                c["x"] = c["x"] + _bdot(c["x"], bd(c["p"]))
            if level < 2:
                for c in ch:
                    c["p"] = _bdot(c["p"], bd(c["p"]))
        for inner, outer in ((blk16, blk32), (blk32, None)):
            keep = (~inner) if outer is None else (outer & (~inner))
            for c in ch:
                c["t"] = _bdot(c["x"], bd(jnp.where(keep, c["tri"], 0.0)))
            for c in ch:
                c["x"] = c["x"] - _bdot(c["t"], bd(c["x"]))
        for c in ch:
            c["wu"] = _bdot(c["x"], jnp.concatenate([bd(c["kt"]), bd(c["av"][:CHUNK])], axis=1))
        for c in ch:
            wm, uv = c["wu"][:, :P2], c["wu"][:, P2:]
            corr = _bdot(c["a_b"][CHUNK:], jnp.concatenate([bd(wm), bd(uv)], axis=1))
            d, q, cc = c["d"], c["q"], c["cc"]
            ry_ref[d, q, cc] = (c["rt"] - corr[:, :P2]).astype(BF16)
            y_ref[d, c["rows"], c["ql"]] = c["av"][CHUNK:] - corr[:, P2:]
            w2_ref[d, q, cc] = jnp.where(same_head, _bdot_tn(wm, c["bh"]), 0.0).astype(BF16)
            full = _bdot_tn(jnp.concatenate([c["v"], -uv], axis=0), jnp.concatenate([c["kh"], c["bh"]], axis=0))
            of_ref[d, q, cc] = jnp.where(left, full[:HEAD], full[HEAD:])
        return carry

    lax.fori_loop(0, n_chunks // WKV_GROUP, prepare, 0)

    def advance(c, states):
        new = []
        for d in range(2):
            cc = c if d == 0 else n_chunks - 1 - c
            rows = pl.ds(pl.multiple_of(cc * CHUNK, CHUNK), CHUNK)
            p_end = pc_ref[d, cc]
            for q, ql in enumerate(lanes):
                s = states[d * WKV_PAIRS + q]
                y_ref[d, rows, ql] = y_ref[d, rows, ql] + _bdot_nt(ry_ref[d, q, cc], bd(s))
                new.append(s * p_end[:, ql] - _bdot(s, w2_ref[d, q, cc]) + of_ref[d, q, cc])
        return tuple(new)

    if has_s0:
        init = tuple(jnp.concatenate([s0_ref[d, 2 * q], s0_ref[d, 2 * q + 1]], axis=1)
                     for d in range(2) for q in range(WKV_PAIRS))
    else:
        init = tuple(jnp.zeros((HEAD, P2), F32) for _ in range(2 * WKV_PAIRS))
    final = lax.fori_loop(0, n_chunks, advance, init)

    if emit_state:
        for d in range(2):
            for q in range(WKV_PAIRS):
                s = final[d * WKV_PAIRS + q]
                sf_ref[d, 2 * q] = s[:, :HEAD]
                sf_ref[d, 2 * q + 1] = s[:, HEAD:]

    EP = 256
    lnx_w, lnx_b = lw_ref[...], lb_ref[...]

    def epilogue(i, carry):
        rows = pl.ds(pl.multiple_of(i * EP, EP), EP)
        r2, k2, v2 = rkv_ref[0, rows, :], rkv_ref[1, rows, :], rkv_ref[2, rows, :]
        y2 = y_ref[0, rows, :] + y_ref[1, rows, :]
        coef = 2.0 + (_sigmoid(al_ref[0, rows, :]) + _sigmoid(al_ref[1, rows, :]) - 2.0) * k_a
        rkr = r2 * k2 * coef * r_k
        yn, bonus = [], []
        for sl in heads:
            y = y2[:, sl]
            mean = jnp.mean(y, axis=-1, keepdims=True)
            var = jnp.mean(jnp.square(y - mean), axis=-1, keepdims=True)
            yn.append((y - mean) * lax.rsqrt(var + RW_GN_EPS))
            bonus.append(jnp.sum(rkr[:, sl], axis=-1, keepdims=True) * v2[:, sl])
        z_ref[rows, :] = jnp.concatenate(yn, axis=1) * lnx_w + lnx_b + jnp.concatenate(bonus, axis=1)
        return carry

    lax.fori_loop(0, seq // EP, epilogue, 0)


def _wkvp(rkv, wl, al, s0, k_k, k_a, r_k, lnx_w, lnx_b, *, batch, seq, row_block0, emit_state):
    has_s0 = s0 is not None
    width = WKV_PAIRS * 2 * HEAD
    steps = D // width
    n_chunks = seq // CHUNK
    tok = lambda lead: pl.BlockSpec((lead, seq, width), lambda b, p: (0, row_block0 + b, p))
    vec = pl.BlockSpec((1, width), lambda b, p: (0, p))
    st = pl.BlockSpec((None, 2, WKV_PAIRS * 2, HEAD, HEAD), lambda b, p: (b, 0, p, 0, 0))
    ins, specs = [rkv, wl, al], [tok(3), tok(2), tok(2)]
    if has_s0:
        ins.append(s0)
        specs.append(st)
    ins += [k_k.reshape(1, D), k_a.reshape(1, D), r_k.reshape(1, D), lnx_w.reshape(1, D), lnx_b.reshape(1, D)]
    specs += [vec] * 5
    out_shape = [jax.ShapeDtypeStruct((batch * seq, D), F32)]
    out_specs = [pl.BlockSpec((seq, width), lambda b, p: (b, p))]
    if emit_state:
        out_shape.append(jax.ShapeDtypeStruct((batch, 2, HEADS, HEAD, HEAD), F32))
        out_specs.append(st)
    return pl.pallas_call(
        functools.partial(_wkvp_kernel, seq=seq, has_s0=has_s0, emit_state=emit_state),
        out_shape=out_shape,
        grid=(batch, steps),
        in_specs=specs,
        out_specs=out_specs,
        scratch_shapes=[pltpu.VMEM((2, WKV_PAIRS, n_chunks, 2 * HEAD, 2 * HEAD), BF16),
                        pltpu.VMEM((2, WKV_PAIRS, n_chunks, HEAD, 2 * HEAD), F32),
                        pltpu.VMEM((2, WKV_PAIRS, n_chunks, CHUNK, 2 * HEAD), BF16),
                        pltpu.VMEM((2, n_chunks, 1, width), F32),
                        pltpu.VMEM((2, seq, width), F32)],
        compiler_params=_params(("parallel", "parallel")),
        name="wkv_%d" % seq,
    )(*ins)


def _pool_kernel(x_ref, g_ref, m_ref, w_ref, sc_ref, o_ref):
    x = x_ref[...]
    h = _rms_mod(x, g_ref[...], m_ref[0], m_ref[1])
    seq = jnp.where(pl.program_id(0) < PROMPT_BLOCKS, SEQ, DEC_SEQ)
    t = lax.broadcasted_iota(jnp.int32, (ROW_BLOCK, 1), 0) & (seq - 1)
    gate = m_ref[2]
    scale = sc_ref[...]
    for g, win in enumerate(POOL_WINDOWS):
        half = win // 2
        cols = slice(g * POOL_GROUP, (g + 1) * POOL_GROUP)
        hg = h[:, cols]
        up = lambda z, m: jnp.where(t + m <= seq - 1, pltpu.roll(z, ROW_BLOCK - m, 0), 0.0)
        down = lambda z, m: jnp.where(t - m >= 0, pltpu.roll(z, m, 0), 0.0)
        fwd = hg
        bwd = down(hg, 1)
        m = 1
        while m < half:
            fwd = fwd + up(fwd, m)
            bwd = bwd + down(bwd, m)
            m *= 2
        count = (jnp.minimum(t + half - 1, seq - 1) - jnp.maximum(t - half, 0) + 1).astype(F32)
        pooled = (fwd + bwd) / count - hg
        mixed = _bdot(pooled, w_ref[g]) * scale[:, cols]
        o_ref[:, cols] = x[:, cols] + gate[:, cols] * mixed


def _pool_layer(x, gain, mods, w_pool, scale):
    row = pl.BlockSpec((ROW_BLOCK, D), lambda i: (i, 0))
    n_g = len(POOL_WINDOWS)
    return pl.pallas_call(
        _pool_kernel,
        out_shape=jax.ShapeDtypeStruct((N_TOK, D), F32),
        grid=(N_TOK // ROW_BLOCK,),
        in_specs=[row, pl.BlockSpec((1, D), lambda i: (0, 0)), _mods_spec(ROW_BLOCK),
                  pl.BlockSpec((n_g, POOL_GROUP, POOL_GROUP), lambda i: (0, 0, 0)),
                  pl.BlockSpec((1, D), lambda i: (0, 0))],
        out_specs=row,
        compiler_params=_params(("parallel",)),
        name="pool",
    )(x, gain.reshape(1, D), mods, w_pool, scale.reshape(1, D))


ATTN_PROMPT_HEADS = 8
NA_ROWS = DEC_SEQ // GRID_W
NA_WIN_R = min(NA_WIN_ROWS, NA_ROWS)


def _head_rms(x, gain):
    return x * lax.rsqrt(jnp.mean(x * x, axis=-1, keepdims=True) + NORM_EPS) * gain


def _attn_prompt_kernel(q_ref, k_ref, v_ref, qg_ref, kg_ref, o_ref, kn_ref):
    heads = [slice(hh * HEAD, (hh + 1) * HEAD) for hh in range(q_ref.shape[1] // HEAD)]
    q = [_head_rms(q_ref[:, sl], qg_ref[...]) * (HEAD ** -0.5) for sl in heads]
    k = [_head_rms(k_ref[:, sl], kg_ref[...]) for sl in heads]
    s = [_bdot_nt(qh, kh) for qh, kh in zip(q, k)]
    m = [jnp.max(sh, axis=-1, keepdims=True) for sh in s]
    p = [jnp.exp(sh - mh) for sh, mh in zip(s, m)]
    den = [jnp.sum(ph, axis=-1, keepdims=True) for ph in p]
    o = [_bdot(ph, v_ref[:, sl]) for ph, sl in zip(p, heads)]
    o_ref[...] = jnp.concatenate([oh / dh for oh, dh in zip(o, den)], axis=1)
    kn_ref[...] = jnp.concatenate(k, axis=1)


def _attn_latent_kernel(q_ref, k_ref, v_ref, qg_ref, kg_ref, ck_ref, cv_ref, bias_ref, o_ref):
    heads = [slice(hh * HEAD, (hh + 1) * HEAD) for hh in range(HEADS_PER_STEP)]
    q = [(_head_rms(q_ref[:, sl], qg_ref[...]) * (HEAD ** -0.5)).astype(BF16) for sl in heads]
    k = [_head_rms(k_ref[:, sl], kg_ref[...]).astype(BF16) for sl in heads]
    v = [v_ref[:, sl].astype(BF16) for sl in heads]
    ck = [ck_ref[:, sl].astype(BF16) for sl in heads]
    cv = [cv_ref[:, sl].astype(BF16) for sl in heads]
    units = []
    for hh in range(HEADS_PER_STEP):
        for qr in range(NA_ROWS):
            r0 = min(max(qr - NA_WIN_R // 2, 0), NA_ROWS - NA_WIN_R)
            units.append((hh, qr, r0, slice(qr * GRID_W, (qr + 1) * GRID_W), slice(r0 * GRID_W, (r0 + NA_WIN_R) * GRID_W)))
    s = [_bdot_nt(q[hh][qs], k[hh][ks])
         + jnp.concatenate([bias_ref[hh, r0 + j - qr + NA_WIN_ROWS - 1] for j in range(NA_WIN_R)], axis=1)
         for hh, qr, r0, qs, ks in units]
    s_ctx = [_bdot_nt(q[hh][qs], ck[hh]) for hh, qr, r0, qs, ks in units]
    m = [jnp.maximum(jnp.max(a, axis=-1, keepdims=True), jnp.max(b, axis=-1, keepdims=True)) for a, b in zip(s, s_ctx)]
    p = [jnp.exp(a - mm) for a, mm in zip(s, m)]
    p_ctx = [jnp.exp(b - mm) for b, mm in zip(s_ctx, m)]
    den = [jnp.sum(a, axis=-1, keepdims=True) + jnp.sum(b, axis=-1, keepdims=True) for a, b in zip(p, p_ctx)]
    o = [_bdot(a, v[hh][ks]) + _bdot(b, cv[hh]) for a, b, (hh, qr, r0, qs, ks) in zip(p, p_ctx, units)]
    o = [a / d for a, d in zip(o, den)]
    o_ref[...] = jnp.concatenate([jnp.concatenate(o[hh * NA_ROWS:(hh + 1) * NA_ROWS], axis=0)
                                  for hh in range(HEADS_PER_STEP)], axis=1)


def _attention_prompt(qkv, q_gain, k_gain):
    width = ATTN_PROMPT_HEADS * HEAD
    tok = lambda j: pl.BlockSpec((None, SEQ, width), lambda b, p: (j, b, p))
    gain = pl.BlockSpec((1, HEAD), lambda b, p: (0, 0))
    out = pl.BlockSpec((SEQ, width), lambda b, p: (b, p))
    return pl.pallas_call(
        _attn_prompt_kernel,
        out_shape=[jax.ShapeDtypeStruct((N_PROMPT, D), F32)] * 2,
        grid=(BATCH, D // width),
        in_specs=[tok(0), tok(1), tok(2), gain, gain],
        out_specs=[out, out],
        compiler_params=_params(("parallel", "parallel")),
        name="attn_prompt",
    )(qkv, qkv, qkv, q_gain.reshape(1, HEAD), k_gain.reshape(1, HEAD))


def _attention_latent(qkv, q_gain, k_gain, ck, cv, bias):
    pair = HEADS // HEADS_PER_STEP
    row0 = N_PROMPT // DEC_SEQ
    gain = pl.BlockSpec((1, HEAD), lambda p, b: (0, 0))
    tok = lambda j: pl.BlockSpec((None, DEC_SEQ, LANES), lambda p, b: (j, row0 + b, p))
    ctx = pl.BlockSpec((None, PAST_LEN, LANES), lambda p, b: (b, 0, p))
    n_dr = 2 * NA_WIN_ROWS - 1
    return pl.pallas_call(
        _attn_latent_kernel,
        out_shape=jax.ShapeDtypeStruct((N_LATENT, D), F32),
        grid=(pair, DEC_BATCH),
        in_specs=[tok(0), tok(1), tok(2), gain, gain, ctx, ctx,
                  pl.BlockSpec((HEADS_PER_STEP, n_dr, GRID_W, GRID_W), lambda p, b: (p, 0, 0, 0))],
        out_specs=pl.BlockSpec((DEC_SEQ, LANES), lambda p, b: (b, p)),
        compiler_params=_params(("parallel", "parallel")),
        name="attn_latent",
    )(qkv, qkv, qkv, q_gain.reshape(1, HEAD), k_gain.reshape(1, HEAD), ck, cv, bias)


def _column_bias_table(rpb):
    n_dr, n_dc = rpb.shape[1], rpb.shape[2]
    span = 2 * GRID_W - 1
    left = GRID_W - NA_WIN_COLS
    g = jnp.pad(rpb, ((0, 0), (0, 0), (left, span - n_dc - left)))
    flat = jnp.broadcast_to(g[:, :, None, :], (HEADS, n_dr, GRID_W, span)).reshape(HEADS, n_dr, GRID_W * span)
    table = flat[:, :, GRID_W - 1:GRID_W - 1 + GRID_W * (span - 1)].reshape(HEADS, n_dr, GRID_W, span - 1)
    table = table[..., :GRID_W]
    col = np.arange(GRID_W)
    c0 = np.clip(col - NA_WIN_COLS // 2, 0, GRID_W - NA_WIN_COLS)
    inside = (col[None, :] >= c0[:, None]) & (col[None, :] < c0[:, None] + NA_WIN_COLS)
    return jnp.where(inside[None, None], table, NEG_BIG)


ROUTER_TILE = 512


def _router_kernel(x_ref, g_ref, m_ref, w_ref, b_ref, h_ref, gate_ref):
    h = _rms_mod(x_ref[...], g_ref[...], m_ref[3], m_ref[4])
    _store_token_rows(h_ref, h)
    h_hi = h.astype(BF16)
    h_lo = (h - h_hi.astype(F32)).astype(BF16)
    w = w_ref[...]
    w_hi = w.astype(BF16)
    w_lo = (w - w_hi.astype(F32)).astype(BF16)
    logits = (jnp.dot(h_hi, w_hi, preferred_element_type=F32) + jnp.dot(h_hi, w_lo, preferred_element_type=F32)
              + jnp.dot(h_lo, w_hi, preferred_element_type=F32) + b_ref[...])
    lane = lax.broadcasted_iota(jnp.int32, logits.shape, 1)
    vals = logits
    top0 = None
    den = 0.0
    gates = jnp.full(logits.shape, -1.0, F32)
    for j in range(TOP_K):
        m = jnp.max(vals, axis=-1, keepdims=True)
        first = jnp.min(jnp.where(vals == m, lane, N_EXPERTS), axis=-1, keepdims=True)
        sel = lane == first
        if j == 0:
            top0 = m
        e = jnp.exp(m - top0)
        den = den + e
        gates = jnp.where(sel, e, gates)
        vals = jnp.where(sel, -jnp.inf, vals)
    gate_ref[...] = jnp.where(gates >= 0.0, gates / den, -1.0)


def _router(x, gain, mods, w_router, b_router):
    row = pl.BlockSpec((ROUTER_TILE, D), lambda i: (i, 0))
    return pl.pallas_call(
        _router_kernel,
        out_shape=[jax.ShapeDtypeStruct((N_TOK * ROW_TILES, LANES), F32),
                   jax.ShapeDtypeStruct((N_TOK, N_EXPERTS), F32)],
        grid=(N_TOK // ROUTER_TILE,),
        in_specs=[row, pl.BlockSpec((1, D), lambda i: (0, 0)), _mods_spec(ROUTER_TILE),
                  pl.BlockSpec((D, N_EXPERTS), lambda i: (0, 0)), pl.BlockSpec((1, N_EXPERTS), lambda i: (0, 0))],
        out_specs=[pl.BlockSpec((ROUTER_TILE * ROW_TILES, LANES), lambda i: (i, 0)),
                   pl.BlockSpec((ROUTER_TILE, N_EXPERTS), lambda i: (i, 0))],
        compiler_params=_params(("parallel",)),
        name="router",
    )(x, gain.reshape(1, D), mods, w_router, b_router.reshape(1, N_EXPERTS))


def _store_token_rows(ref, value, lead=()):
    n = value.shape[0]
    for c in range(ROW_TILES):
        ref[lead + (pl.ds(c, n, stride=ROW_TILES), slice(None))] = value[:, c * LANES:(c + 1) * LANES]


def _load_token_chunk(ref, n, c, lead=()):
    return ref[lead + (pl.ds(c, n, stride=ROW_TILES), slice(None))]


def _token_copy(src, src_tok, dst, dst_tok, sem):
    rows = lambda t: pl.ds(t * ROW_TILES if isinstance(t, int) else pl.multiple_of(t * ROW_TILES, ROW_TILES), ROW_TILES)
    return pltpu.make_async_copy(src.at[rows(src_tok)], dst.at[rows(dst_tok)], sem)


def _invert_kernel(pos_ref, pad_ref, dst_ref):
    def clear(i, carry):
        dst_ref[i] = 0
        return carry

    for e in range(N_EXPERTS):
        lax.fori_loop(pad_ref[2 * e], pad_ref[2 * e + 1], clear, 0)
    lax.fori_loop(pad_ref[2 * N_EXPERTS - 1], MOE_ROWS, clear, 0)

    def place(n, carry):
        for j in range(TOP_K):
            dst_ref[pos_ref[n * TOP_K + j]] = j * N_TOK + n
        return carry

    lax.fori_loop(0, N_TOK, place, 0, unroll=4)


def _invert(pos4, pad_bounds):
    smem = pl.BlockSpec(memory_space=pltpu.SMEM)
    return pl.pallas_call(
        _invert_kernel,
        out_shape=jax.ShapeDtypeStruct((MOE_ROWS,), jnp.int32),
        in_specs=[smem, smem],
        out_specs=smem,
        name="invert",
    )(pos4, pad_bounds)


def _experts_kernel(te_ref, tv_ref, nt_ref, dst_ref, h_hbm, wgu_ref, bgu_ref, wd_ref, bd_ref, ys_hbm,
                    xbuf, ybuf, sem_in, sem_out, wgu_bf, wd_bf, act_ref):
    t = pl.program_id(0)
    n_tiles = nt_ref[0]
    slot = t % 2
    other = 1 - slot

    def gather_row(tile, r, to_slot):
        tok = dst_ref[tile * MOE_TILE + r] & (N_TOK - 1)
        return _token_copy(h_hbm, tok, xbuf.at[to_slot], r, sem_in.at[to_slot])

    def scatter_row(tile, r, from_slot):
        return _token_copy(ybuf.at[from_slot], r, ys_hbm, dst_ref[tile * MOE_TILE + r], sem_out.at[from_slot])

    def gather_wait(s):
        pltpu.make_async_copy(h_hbm.at[pl.ds(0, MOE_TILE * ROW_TILES)], xbuf.at[s], sem_in.at[s]).wait()

    def scatter_wait(s, tokens):
        @pl.when(tokens > 0)
        def _():
            rows = pl.ds(0, pl.multiple_of(tokens * ROW_TILES, ROW_TILES))
            pltpu.make_async_copy(ybuf.at[s].at[rows], ys_hbm.at[rows], sem_out.at[s]).wait()

    @pl.when(t == 0)
    def _():
        def issue(r, carry):
            gather_row(0, r, 0).start()
            return carry

        lax.fori_loop(0, MOE_TILE, issue, 0)

    @pl.when(t < n_tiles)
    def _():
        @pl.when((t == 0) | (te_ref[t] != te_ref[jnp.maximum(t - 1, 0)]))
        def _():
            wgu_bf[...] = wgu_ref[...].astype(BF16)
            wd_bf[...] = wd_ref[...].astype(BF16)

        gather_wait(slot)
        nxt = jnp.minimum(t + 1, n_tiles - 1)
        prev, prev_rows = jnp.maximum(t - 1, 0), tv_ref[t]
        n_up, n_down = D_EXPERT // MOE_COL, D // MOE_COL

        def issue_gathers(block):
            for r in range(block * (MOE_TILE // n_up), (block + 1) * (MOE_TILE // n_up)):
                gather_row(nxt, r, other).start()

        def issue_scatters(block):
            for r in range(block * (MOE_TILE // n_down), (block + 1) * (MOE_TILE // n_down)):
                @pl.when(r < prev_rows)
                def _():
                    scatter_row(prev, r, other).start()

        x = jnp.concatenate([_load_token_chunk(xbuf, MOE_TILE, c, (slot,)) for c in range(ROW_TILES)],
                            axis=1).astype(BF16)
        for c in range(n_up):
            cols = slice(c * MOE_COL, (c + 1) * MOE_COL)
            ups = slice(D_EXPERT + c * MOE_COL, D_EXPERT + (c + 1) * MOE_COL)
            glu = jnp.dot(x, wgu_bf[:, cols], preferred_element_type=F32) + bgu_ref[:, cols]
            lin = jnp.dot(x, wgu_bf[:, ups], preferred_element_type=F32) + bgu_ref[:, ups]
            glu = jnp.minimum(glu, SWIGLU_LIMIT)
            lin = jnp.clip(lin, -SWIGLU_LIMIT, SWIGLU_LIMIT)
            act_ref[:, cols] = (glu * _sigmoid(SWIGLU_ALPHA * glu) * (lin + 1.0)).astype(BF16)
            issue_gathers(c)
        scatter_wait(slot, jnp.where(t >= 2, tv_ref[jnp.maximum(t - 1, 0)], 0))
        act = act_ref[...]
        for c in range(n_down):
            cols = slice(c * MOE_COL, (c + 1) * MOE_COL)
            y = jnp.dot(act, wd_bf[:, cols], preferred_element_type=F32) + bd_ref[:, cols]
            for k in range(MOE_COL // LANES):
                chunk = c * (MOE_COL // LANES) + k
                ybuf[slot, pl.ds(chunk, MOE_TILE, stride=ROW_TILES), :] = y[:, k * LANES:(k + 1) * LANES]
            issue_scatters(c)

    @pl.when(t == n_tiles - 1)
    def _():
        def issue(r, carry):
            scatter_row(t, r, slot).start()
            return carry

        lax.fori_loop(0, tv_ref[t + 1], issue, 0)
        gather_wait(other)
        scatter_wait(other, tv_ref[t])
        scatter_wait(slot, tv_ref[t + 1])


def _experts(h, tile_expert, tile_rows, n_tiles, dst, layer, w_gu, b_gu, w_down, b_down):
    weight = lambda shape: pl.BlockSpec((None, None) + shape, lambda t, te, tv, nt, dst: (layer, te[t], 0, 0))
    grid_spec = pltpu.PrefetchScalarGridSpec(
        num_scalar_prefetch=4,
        grid=(MOE_TILES,),
        in_specs=[pl.BlockSpec(memory_space=pl.ANY),
                  weight((D, 2 * D_EXPERT)), weight((1, 2 * D_EXPERT)), weight((D_EXPERT, D)), weight((1, D))],
        out_specs=pl.BlockSpec(memory_space=pl.ANY),
        scratch_shapes=[pltpu.VMEM((2, MOE_TILE * ROW_TILES, LANES), F32),
                        pltpu.VMEM((2, MOE_TILE * ROW_TILES, LANES), F32),
                        pltpu.SemaphoreType.DMA((2,)), pltpu.SemaphoreType.DMA((2,)),
                        pltpu.VMEM((D, 2 * D_EXPERT), BF16), pltpu.VMEM((D_EXPERT, D), BF16),
                        pltpu.VMEM((MOE_TILE, D_EXPERT), BF16)],
    )
    return pl.pallas_call(
        _experts_kernel,
        out_shape=jax.ShapeDtypeStruct((TOP_K * N_TOK * ROW_TILES, LANES), F32),
        grid_spec=grid_spec,
        compiler_params=_params(("arbitrary",)),
        name="experts",
    )(tile_expert, tile_rows, n_tiles, dst, h, w_gu, b_gu.reshape(DEPTH, N_EXPERTS, 1, 2 * D_EXPERT),
      w_down, b_down.reshape(DEPTH, N_EXPERTS, 1, D))


def _combine_kernel(y0_ref, y1_ref, y2_ref, y3_ref, gate_ref, x_ref, m_ref, o_ref):
    gate = gate_ref[...]
    g = [gate[:, j:j + 1] for j in range(TOP_K)]
    scale = m_ref[5]
    for c in range(ROW_TILES):
        cols = slice(c * LANES, (c + 1) * LANES)
        y = [_load_token_chunk(ref, COMBINE_TILE, c) for ref in (y0_ref, y1_ref, y2_ref, y3_ref)]
        moe = (y[0] * g[0] + y[1] * g[1]) + (y[2] * g[2] + y[3] * g[3])
        o_ref[:, cols] = x_ref[:, cols] + scale[:, cols] * moe


def _combine(ys, gate4, x, mods):
    row = pl.BlockSpec((COMBINE_TILE, D), lambda i: (i, 0))
    blocks = N_TOK // COMBINE_TILE
    choice = lambda j: pl.BlockSpec((COMBINE_TILE * ROW_TILES, LANES), lambda i: (j * blocks + i, 0))
    return pl.pallas_call(
        _combine_kernel,
        out_shape=jax.ShapeDtypeStruct((N_TOK, D), F32),
        grid=(blocks,),
        in_specs=[choice(j) for j in range(TOP_K)]
        + [pl.BlockSpec((COMBINE_TILE, LANES), lambda i: (i, 0)), row, _mods_spec(COMBINE_TILE)],
        out_specs=row,
        compiler_params=_params(("parallel",)),
        name="combine",
    )(ys, ys, ys, ys, gate4, x, mods)


def _moe_layer(x, gain, mods, layer, w_router, b_router, w_gu, b_gu, w_down, b_down):
    h, gates = _router(x, gain, mods, w_router, b_router)
    sel = gates >= 0.0
    sel_i = sel.astype(jnp.int32)
    rank = jnp.cumsum(sel_i, axis=0) - sel_i
    count = jnp.sum(sel_i, axis=0)
    padded = ((count + MOE_TILE - 1) // MOE_TILE) * MOE_TILE
    group_end = jnp.cumsum(padded)
    pos = group_end[None, :] - padded[None, :] + rank
    slot = jnp.cumsum(sel_i, axis=1) - 1
    pick = [sel & (slot == j) for j in range(TOP_K)]
    pos4 = jnp.stack([jnp.sum(jnp.where(m, pos, 0), axis=1) for m in pick], axis=1).astype(jnp.int32)
    gate4 = jnp.stack([jnp.sum(jnp.where(m, gates, 0.0), axis=1) for m in pick], axis=1)
    gate4 = jnp.pad(gate4, ((0, 0), (0, LANES - TOP_K)))
    pos4 = pos4.reshape(-1)
    n_tiles = (group_end[-1] // MOE_TILE).astype(jnp.int32)
    tile_start = jnp.minimum(jnp.arange(MOE_TILES, dtype=jnp.int32) * MOE_TILE, group_end[-1] - 1)
    tile_expert = jnp.sum((group_end[None, :] <= tile_start[:, None]).astype(jnp.int32), axis=1)
    tile_expert = jnp.minimum(tile_expert, N_EXPERTS - 1).astype(jnp.int32)
    real_end = group_end - padded + count
    tile_rows = jnp.clip(real_end[tile_expert] - jnp.arange(MOE_TILES, dtype=jnp.int32) * MOE_TILE, 0, MOE_TILE)
    tile_rows = jnp.concatenate([jnp.zeros((1,), jnp.int32), tile_rows.astype(jnp.int32)])
    pad_bounds = jnp.stack([real_end, group_end], axis=1).reshape(-1).astype(jnp.int32)
    dst = _invert(pos4, pad_bounds)
    ys = _experts(h, tile_expert, tile_rows, n_tiles.reshape(1), dst, layer, w_gu, b_gu, w_down, b_down)
    return _combine(ys, gate4, x, mods)


def _rwkv_layer(x, gain, mods, state, p):
    (mu, w_rkv, w_out, w0, w1, w2, a0, a1, a2, g1, g2, k_k, k_a, r_k, lnx_w, lnx_b) = p
    h, xx = _normmod(x, gain, mods, with_xx=True)
    mu = mu.reshape(N_MOD, 1, D)
    rkv = _proj3(h, w_rkv, xx, mu[jnp.array([0, 2, 3])])
    wl = _lora(h, xx, mu[jnp.array([1, 1])], w1, w2, w0.reshape(2, 1, D), "tanh")
    al = _lora(h, xx, mu[jnp.array([4, 4])], a1, a2, a0.reshape(2, 1, D), "none")
    g = _lora(h, xx, mu[5:6], g1[None], g2[None], None, "sigmoid")[0]
    vecs = (k_k, k_a, r_k, lnx_w, lnx_b)
    z_p, s_ctx = _wkvp(rkv, wl, al, None, *vecs, batch=BATCH, seq=SEQ, row_block0=0, emit_state=True)
    (z_s,) = _wkvp(rkv, wl, al, state, *vecs, batch=DEC_BATCH, seq=DEC_SEQ,
                  row_block0=N_PROMPT // DEC_SEQ, emit_state=False)
    z = jnp.concatenate([z_p, z_s], axis=0)
    return _mmres(z, w_out, x, mods, g=g), s_ctx


def _na_layer(x, gain, mods, ck, cv, p):
    w_qkv, w_out, q_norm, k_norm, rpb = p
    (h,) = _normmod(x, gain, mods, with_xx=False)
    qkv = _proj3(h, w_qkv)
    o_p, k_p = _attention_prompt(qkv, q_norm, k_norm)
    o_s = _attention_latent(qkv, q_norm, k_norm, ck.reshape(DEC_BATCH, PAST_LEN, D),
                            cv.reshape(DEC_BATCH, PAST_LEN, D), _column_bias_table(rpb))
    o = jnp.concatenate([o_p, o_s], axis=0)
    new_k = k_p.reshape(BATCH, SEQ, HEADS, HEAD)
    new_v = qkv[2, :N_PROMPT].reshape(BATCH, SEQ, HEADS, HEAD)
    return _mmres(o, w_out, x, mods), new_k, new_v


def kernel(x_prompt, x_sample, c, c_ctx, state_wkv, cache_k, cache_v, norm_mix, norm_ffn, w_mod, b_mod, rw_mu, rw_w_rkv, rw_w_out, rw_w0, rw_w1, rw_w2, rw_a0, rw_a1, rw_a2, rw_g1, rw_g2, rw_k_k, rw_k_a, rw_r_k, rw_lnx_w, rw_lnx_b, pool_w, pool_scale, na_w_qkv, na_w_out, na_q_norm, na_k_norm, na_rpb, moe_w_router, moe_b_router, moe_w_gu, moe_b_gu, moe_w_down, moe_b_down):
    x = jnp.concatenate([x_prompt.reshape(N_PROMPT, D), x_sample.reshape(N_LATENT, D)], axis=0)
    cond = jnp.concatenate([c_ctx[None, :], c, jnp.zeros((COND_ROWS - 1 - DEC_BATCH, D), F32)], axis=0)
    mods = _adaln(cond, w_mod, b_mod)
    new_wkv, new_k, new_v = [], [], []
    for i in range(DEPTH):
        kind, slot = i % N_MIXERS, i // N_MIXERS
        if kind == 0:
            rw = (rw_mu[slot], rw_w_rkv[slot], rw_w_out[slot], rw_w0[slot], rw_w1[slot], rw_w2[slot],
                  rw_a0[slot], rw_a1[slot], rw_a2[slot], rw_g1[slot], rw_g2[slot], rw_k_k[slot],
                  rw_k_a[slot], rw_r_k[slot], rw_lnx_w[slot], rw_lnx_b[slot])
            x, s_ctx = _rwkv_layer(x, norm_mix[i], mods[i], state_wkv[:, slot], rw)
            new_wkv.append(s_ctx)
        elif kind == 1:
            x = _pool_layer(x, norm_mix[i], mods[i], pool_w[slot], pool_scale[slot])
        else:
            na = (na_w_qkv[slot], na_w_out[slot], na_q_norm[slot], na_k_norm[slot], na_rpb[slot])
            x, k_p, v_p = _na_layer(x, norm_mix[i], mods[i], cache_k[:, slot], cache_v[:, slot], na)
            new_k.append(k_p)
            new_v.append(v_p)
        x = _moe_layer(x, norm_ffn[i], mods[i], i, moe_w_router[i], moe_b_router[i], moe_w_gu, moe_b_gu,
                       moe_w_down, moe_b_down)
    y_prompt = x[:N_PROMPT].reshape(BATCH, SEQ, D)
    y_sample = x[N_PROMPT:].reshape(DEC_BATCH, DEC_SEQ, D)
    return (y_prompt, y_sample, jnp.stack(new_wkv, axis=1), jnp.stack(new_k, axis=1), jnp.stack(new_v, axis=1))
```

```python
import functools

import jax
import jax.numpy as jnp
import numpy as np
from jax import lax
from jax.experimental import pallas as pl
from jax.experimental.pallas import tpu as pltpu

F32 = jnp.float32
BF16 = jnp.bfloat16

D = 1024
BATCH, SEQ = 16, 256
DEC_BATCH, DEC_SEQ = 4, 1024
DEPTH = 4
PAST_LEN = 512
GRID_W = 64
N_MIXERS = 3
N_MOD = 6
NORM_EPS = 1e-6
HEAD = 64
HEADS = D // HEAD
RW_GN_EPS = 64e-5
POOL_WINDOWS = (2, 4, 8, 16)
POOL_GROUP = D // len(POOL_WINDOWS)
NA_WIN_ROWS, NA_WIN_COLS = 8, 16
N_EXPERTS, TOP_K = 32, 4
D_EXPERT = D
SWIGLU_LIMIT, SWIGLU_ALPHA = 7.0, 1.702

N_PROMPT = BATCH * SEQ
N_LATENT = DEC_BATCH * DEC_SEQ
N_TOK = N_PROMPT + N_LATENT
COND_ROWS = 8
ROW_BLOCK = 1024
PROMPT_BLOCKS = N_PROMPT // ROW_BLOCK
LANES = 128
SUBLANES = 8
ROW_TILES = D // LANES
assert ROW_TILES == SUBLANES
HEADS_PER_STEP = LANES // HEAD
CHUNK = 64
WKV_GROUP = 4
WKV_PAIRS = 2
MOE_TILE = 256
MOE_ROWS = N_TOK * TOP_K + N_EXPERTS * MOE_TILE
MOE_TILES = MOE_ROWS // MOE_TILE
MOE_COL = 256
COMBINE_TILE = 256
NEG_BIG = -1e30
VMEM_LIMIT = 56 * 1024 * 1024


def _cond_row(block_1024):
    return jnp.maximum(block_1024 - (PROMPT_BLOCKS - 1), 0)


def _mods_spec(rows_per_block):
    per = ROW_BLOCK // rows_per_block
    return pl.BlockSpec((None, N_MOD, 1, D), lambda *ids: (_cond_row(ids[-1] // per), 0, 0, 0))


def _params(sem):
    return pltpu.CompilerParams(dimension_semantics=sem, vmem_limit_bytes=VMEM_LIMIT)


def _bdot(a, b):
    return jnp.dot(a.astype(BF16), b.astype(BF16), preferred_element_type=F32)


def _bdot_nt(a, b):
    return lax.dot_general(a.astype(BF16), b.astype(BF16), (((1,), (1,)), ((), ())),
                           preferred_element_type=F32)


def _bdot_tn(a, b):
    return lax.dot_general(a.astype(BF16), b.astype(BF16), (((0,), (0,)), ((), ())),
                           preferred_element_type=F32)


def _split3(x):
    hi = x.astype(BF16)
    r1 = x - hi.astype(F32)
    mid = r1.astype(BF16)
    lo = (r1 - mid.astype(F32)).astype(BF16)
    return hi, mid, lo


def _sigmoid(x):
    return 1.0 / (1.0 + jnp.exp(-x))


def _rms_mod(x, gain, shift, scale):
    y = x * lax.rsqrt(jnp.mean(x * x, axis=-1, keepdims=True) + NORM_EPS)
    return (y * gain) * (1.0 + scale) + shift


def _adaln_kernel(c_ref, w_ref, b_ref, o_ref):
    c = c_ref[...]
    s = c * _sigmoid(c)
    s_hi = s.astype(BF16)
    s_lo = (s - s_hi.astype(F32)).astype(BF16)
    w = w_ref[...]
    w_hi = w.astype(BF16)
    w_lo = (w - w_hi.astype(F32)).astype(BF16)
    o_ref[...] = (jnp.dot(s_hi, w_hi, preferred_element_type=F32) + jnp.dot(s_lo, w_hi, preferred_element_type=F32)
                  + jnp.dot(s_hi, w_lo, preferred_element_type=F32) + b_ref[...])


def _adaln(cond, w_mod, b_mod):
    out = pl.pallas_call(
        _adaln_kernel,
        out_shape=jax.ShapeDtypeStruct((DEPTH, COND_ROWS, N_MOD * D), F32),
        grid=(DEPTH, N_MOD),
        in_specs=[pl.BlockSpec((COND_ROWS, D), lambda l, j: (0, 0)),
                  pl.BlockSpec((None, D, D), lambda l, j: (l, 0, j)),
                  pl.BlockSpec((None, 1, D), lambda l, j: (l, 0, j))],
        out_specs=pl.BlockSpec((None, COND_ROWS, D), lambda l, j: (l, 0, j)),
        compiler_params=_params(("parallel", "parallel")),
        name="adaln",
    )(cond, w_mod, b_mod.reshape(DEPTH, 1, N_MOD * D))
    return out.reshape(DEPTH, COND_ROWS, N_MOD, 1, D)


def _normmod_kernel(x_ref, g_ref, m_ref, h_ref, *xx_ref, shift_idx, scale_idx):
    h = _rms_mod(x_ref[...], g_ref[...], m_ref[shift_idx], m_ref[scale_idx])
    h_ref[...] = h
    if xx_ref:
        seq = jnp.where(pl.program_id(0) < PROMPT_BLOCKS, SEQ, DEC_SEQ)
        t = lax.broadcasted_iota(jnp.int32, (ROW_BLOCK, 1), 0) & (seq - 1)
        prev = jnp.where(t == 0, 0.0, pltpu.roll(h, 1, 0))
        nxt = jnp.where(t == seq - 1, 0.0, pltpu.roll(h, ROW_BLOCK - 1, 0))
        xx_ref[0][...] = 0.5 * (prev + nxt) - h


def _normmod(x, gain, mods, with_xx):
    n_out = 2 if with_xx else 1
    row = pl.BlockSpec((ROW_BLOCK, D), lambda i: (i, 0))
    outs = pl.pallas_call(
        functools.partial(_normmod_kernel, shift_idx=0, scale_idx=1),
        out_shape=[jax.ShapeDtypeStruct((N_TOK, D), F32)] * n_out,
        grid=(N_TOK // ROW_BLOCK,),
        in_specs=[row, pl.BlockSpec((1, D), lambda i: (0, 0)), _mods_spec(ROW_BLOCK)],
        out_specs=[row] * n_out,
        compiler_params=_params(("parallel",)),
        name="normmod",
    )(x, gain.reshape(1, D), mods)
    return outs


PROJ_TILE = 512


def _proj3_kernel(*refs, mix):
    if mix:
        h_ref, xx_ref, mu_ref, w_ref, o_ref, wbf_ref = refs
    else:
        h_ref, w_ref, o_ref, wbf_ref = refs

    @pl.when(pl.program_id(1) == 0)
    def _():
        wbf_ref[...] = w_ref[...].astype(BF16)

    x = h_ref[...]
    if mix:
        x = x + xx_ref[...] * mu_ref[...]
    o_ref[...] = jnp.dot(x.astype(BF16), wbf_ref[...], preferred_element_type=F32)


def _proj3(h, w, xx=None, mu=None):
    mix = xx is not None
    row = pl.BlockSpec((PROJ_TILE, D), lambda j, i: (i, 0))
    if mix:
        ins = [h, xx, mu, w]
        specs = [row, row, pl.BlockSpec((None, 1, D), lambda j, i: (j, 0, 0)),
                 pl.BlockSpec((None, D, D), lambda j, i: (j, 0, 0))]
    else:
        ins = [h, w]
        specs = [row, pl.BlockSpec((D, D), lambda j, i: (0, j))]
    return pl.pallas_call(
        functools.partial(_proj3_kernel, mix=mix),
        out_shape=jax.ShapeDtypeStruct((3, N_TOK, D), F32),
        grid=(3, N_TOK // PROJ_TILE),
        in_specs=specs,
        out_specs=pl.BlockSpec((None, PROJ_TILE, D), lambda j, i: (j, i, 0)),
        scratch_shapes=[pltpu.VMEM((D, D), BF16)],
        compiler_params=_params(("arbitrary", "arbitrary")),
        name="proj3",
    )(*ins)


def _lora_kernel(h_ref, xx_ref, mu_ref, a_ref, b_ref, *rest, n, act, has_bias):
    if has_bias:
        bias_ref, o_ref = rest
    else:
        (o_ref,) = rest
    h = h_ref[...]
    xx = xx_ref[...]
    for j in range(n):
        x = h + xx * mu_ref[j]
        t = _bdot(x, a_ref[j])
        if act == "tanh":
            t = jnp.tanh(t)
        elif act == "sigmoid":
            t = _sigmoid(t)
        o = _bdot(t, b_ref[j])
        if has_bias:
            o = o + bias_ref[j]
        o_ref[j] = o


def _lora(h, xx, mu, a, b, bias, act):
    n, _, r = a.shape
    row = pl.BlockSpec((PROJ_TILE, D), lambda i: (i, 0))
    full = lambda shape: pl.BlockSpec(shape, lambda i: (0,) * len(shape))
    ins = [h, xx, mu, a, b]
    specs = [row, row, full((n, 1, D)), full((n, D, r)), full((n, r, D))]
    if bias is not None:
        ins.append(bias)
        specs.append(full((n, 1, D)))
    return pl.pallas_call(
        functools.partial(_lora_kernel, n=n, act=act, has_bias=bias is not None),
        out_shape=jax.ShapeDtypeStruct((n, N_TOK, D), F32),
        grid=(N_TOK // PROJ_TILE,),
        in_specs=specs,
        out_specs=pl.BlockSpec((n, PROJ_TILE, D), lambda i: (0, i, 0)),
        compiler_params=_params(("parallel",)),
        name="lora_" + act,
    )(*ins)


def _mmres_kernel(*refs, with_g, gate_idx):
    if with_g:
        z_ref, g_ref, w_ref, x_ref, m_ref, o_ref, wbf_ref = refs
    else:
        z_ref, w_ref, x_ref, m_ref, o_ref, wbf_ref = refs

    @pl.when(pl.program_id(0) == 0)
    def _():
        wbf_ref[...] = w_ref[...].astype(BF16)

    z = z_ref[...]
    if with_g:
        z = z * g_ref[...]
    o_ref[...] = x_ref[...] + m_ref[gate_idx] * jnp.dot(z.astype(BF16), wbf_ref[...],
                                                        preferred_element_type=F32)


def _mmres(z, w, x, mods, g=None):
    row = pl.BlockSpec((PROJ_TILE, D), lambda i: (i, 0))
    ins, specs = [z], [row]
    if g is not None:
        ins.append(g)
        specs.append(row)
    ins += [w, x, mods]
    specs += [pl.BlockSpec((D, D), lambda i: (0, 0)), row, _mods_spec(PROJ_TILE)]
    return pl.pallas_call(
        functools.partial(_mmres_kernel, with_g=g is not None, gate_idx=2),
        out_shape=jax.ShapeDtypeStruct((N_TOK, D), F32),
        grid=(N_TOK // PROJ_TILE,),
        in_specs=specs,
        out_specs=row,
        scratch_shapes=[pltpu.VMEM((D, D), BF16)],
        compiler_params=_params(("arbitrary",)),
        name="mmres",
    )(*ins)


def _block_diag(x, left):
    return jnp.concatenate([jnp.where(left, x, 0.0), jnp.where(left, 0.0, x)], axis=0)


def _wkvp_kernel(*refs, seq, has_s0, emit_state):
    it = iter(refs)
    rkv_ref, wl_ref, al_ref = next(it), next(it), next(it)
    s0_ref = next(it) if has_s0 else None
    kk_ref, ka_ref, rk_ref, lw_ref, lb_ref = next(it), next(it), next(it), next(it), next(it)
    z_ref = next(it)
    sf_ref = next(it) if emit_state else None
    w2_ref, of_ref, ry_ref, pc_ref, y_ref = next(it), next(it), next(it), next(it), next(it)
    n_chunks = seq // CHUNK
    P2 = 2 * HEAD

    ri = lax.broadcasted_iota(jnp.int32, (CHUNK, CHUNK), 0)
    ci = lax.broadcasted_iota(jnp.int32, (CHUNK, CHUNK), 1)
    tri_bf = ((ri >= ci).astype(BF16), (ri <= ci).astype(BF16))
    rp = lax.broadcasted_iota(jnp.int32, (CHUNK, P2), 0)
    lane = lax.broadcasted_iota(jnp.int32, (CHUNK, P2), 1)
    cp = lane & (HEAD - 1)
    left = lane < HEAD
    eye = (rp == cp).astype(F32)
    blk16 = (rp // 16) == (cp // 16)
    blk32 = (rp // 32) == (cp // 32)
    mask2 = (jnp.concatenate([rp > cp, rp >= cp], axis=0), jnp.concatenate([rp < cp, rp <= cp], axis=0))
    left2 = jnp.concatenate([left, left], axis=0)
    same_head = left2 == (lax.broadcasted_iota(jnp.int32, (2 * CHUNK, P2), 0) < HEAD)
    k_k, k_a, r_k = kk_ref[...], ka_ref[...], rk_ref[...]
    heads = tuple(slice(hh * HEAD, (hh + 1) * HEAD) for hh in range(WKV_PAIRS * 2))
    lanes = tuple(slice(q * P2, (q + 1) * P2) for q in range(WKV_PAIRS))
    bd = lambda x: _block_diag(x, left)

    def prepare(g, carry):
        ch = []
        for j in range(WKV_GROUP):
            cc = g * WKV_GROUP + j
            rows = pl.ds(pl.multiple_of(cc * CHUNK, CHUNK), CHUNK)
            r2, k2, v2 = rkv_ref[0, rows, :], rkv_ref[1, rows, :], rkv_ref[2, rows, :]
            kk2 = k2 * k_k
            kk2 = jnp.concatenate(
                [kk2[:, sl] / jnp.maximum(jnp.sqrt(jnp.sum(kk2[:, sl] * kk2[:, sl], axis=-1, keepdims=True)), 1e-12)
                 for sl in heads], axis=1)
            for d in range(2):
                w_in = -wl_ref[d, rows, :]
                softplus = jnp.maximum(w_in, 0.0) + jnp.log(1.0 + jnp.exp(-jnp.abs(w_in)))
                logdec = -jnp.exp(-softplus - 0.5)
                a2 = _sigmoid(al_ref[d, rows, :])
                hi, mid, lo = _split3(logdec)
                lp = (jnp.dot(tri_bf[d], hi, preferred_element_type=F32)
                      + jnp.dot(tri_bf[d], mid, preferred_element_type=F32)
                      + jnp.dot(tri_bf[d], lo, preferred_element_type=F32))
                lp_end = lp[CHUNK - 1:CHUNK, :] if d == 0 else lp[0:1, :]
                e_neg, p_end = jnp.exp(-lp), jnp.exp(lp_end)
                kt2 = kk2 * jnp.exp(lp - logdec)
                rt2 = r2 * jnp.exp(lp)
                ks2 = k2 * (1.0 + (a2 - 1.0) * k_a) * e_neg
                bs2 = kk2 * a2 * e_neg
                kh2, bh2 = ks2 * p_end, bs2 * p_end
                pc_ref[d, cc] = p_end
                for q, ql in enumerate(lanes):
                    ch.append(dict(cc=cc, rows=rows, d=d, q=q, ql=ql, kt=kt2[:, ql], rt=rt2[:, ql], ks=ks2[:, ql],
                                   bs=bs2[:, ql], kh=kh2[:, ql], bh=bh2[:, ql], v=v2[:, ql]))
        for c in ch:
            q2 = jnp.concatenate([c["kt"], c["rt"]], axis=0)
            c["a_k"] = jnp.where(mask2[c["d"]], _bdot_nt(q2, bd(c["ks"])), 0.0)
            c["a_b"] = jnp.where(mask2[c["d"]], _bdot_nt(q2, bd(c["bs"])), 0.0)
        for c in ch:
            c["av"] = _bdot(c["a_k"], bd(c["v"]))
        for c in ch:
            c["tri"] = c["a_b"][:CHUNK]
            d16 = jnp.where(blk16, c["tri"], 0.0)
            c["x"] = eye - d16
            c["p"] = _bdot(d16, bd(d16))
        for level in range(3):
            for c in ch:
                c["x"] = c["x"] + _bdot(c["x"], bd(c["p"]))
            if level < 2:
                for c in ch:
                    c["p"] = _bdot(c["p"], bd(c["p"]))
        for inner, outer in ((blk16, blk32), (blk32, None)):
            keep = (~inner) if outer is None else (outer & (~inner))
            for c in ch:
                c["t"] = _bdot(c["x"], bd(jnp.where(keep, c["tri"], 0.0)))
            for c in ch:
                c["x"] = c["x"] - _bdot(c["t"], bd(c["x"]))
        for c in ch:
            c["wu"] = _bdot(c["x"], jnp.concatenate([bd(c["kt"]), bd(c["av"][:CHUNK])], axis=1))
        for c in ch:
            wm, uv = c["wu"][:, :P2], c["wu"][:, P2:]
            corr = _bdot(c["a_b"][CHUNK:], jnp.concatenate([bd(wm), bd(uv)], axis=1))
            d, q, cc = c["d"], c["q"], c["cc"]
            ry_ref[d, q, cc] = (c["rt"] - corr[:, :P2]).astype(BF16)
            y_ref[d, c["rows"], c["ql"]] = c["av"][CHUNK:] - corr[:, P2:]
            w2_ref[d, q, cc] = jnp.where(same_head, _bdot_tn(wm, c["bh"]), 0.0).astype(BF16)
            full = _bdot_tn(jnp.concatenate([c["v"], -uv], axis=0), jnp.concatenate([c["kh"], c["bh"]], axis=0))
            of_ref[d, q, cc] = jnp.where(left, full[:HEAD], full[HEAD:])
        return carry

    lax.fori_loop(0, n_chunks // WKV_GROUP, prepare, 0)

    def advance(c, states):
        new = []
        for d in range(2):
            cc = c if d == 0 else n_chunks - 1 - c
            rows = pl.ds(pl.multiple_of(cc * CHUNK, CHUNK), CHUNK)
            p_end = pc_ref[d, cc]
            for q, ql in enumerate(lanes):
                s = states[d * WKV_PAIRS + q]
                new.append(s * p_end[:, ql] - _bdot(s, w2_ref[d, q, cc]) + of_ref[d, q, cc])
        for d in range(2):
            cc = c if d == 0 else n_chunks - 1 - c
            rows = pl.ds(pl.multiple_of(cc * CHUNK, CHUNK), CHUNK)
            for q, ql in enumerate(lanes):
                y_ref[d, rows, ql] = y_ref[d, rows, ql] + _bdot_nt(ry_ref[d, q, cc], bd(states[d * WKV_PAIRS + q]))
        return tuple(new)

    if has_s0:
        init = tuple(jnp.concatenate([s0_ref[d, 2 * q], s0_ref[d, 2 * q + 1]], axis=1)
                     for d in range(2) for q in range(WKV_PAIRS))
    else:
        init = tuple(jnp.zeros((HEAD, P2), F32) for _ in range(2 * WKV_PAIRS))
    final = lax.fori_loop(0, n_chunks, advance, init)

    if emit_state:
        for d in range(2):
            for q in range(WKV_PAIRS):
                s = final[d * WKV_PAIRS + q]
                sf_ref[d, 2 * q] = s[:, :HEAD]
                sf_ref[d, 2 * q + 1] = s[:, HEAD:]

    EP = 256
    lnx_w, lnx_b = lw_ref[...], lb_ref[...]
    width = WKV_PAIRS * P2
    hr = lax.broadcasted_iota(jnp.int32, (width, width), 0) // HEAD
    hc = lax.broadcasted_iota(jnp.int32, (width, width), 1) // HEAD
    head_ones = (hr == hc).astype(BF16)

    def head_sum(x, passes):
        hi = x.astype(BF16)
        total = jnp.dot(hi, head_ones, preferred_element_type=F32)
        if passes == 2:
            total = total + jnp.dot((x - hi.astype(F32)).astype(BF16), head_ones, preferred_element_type=F32)
        return total

    def epilogue(i, carry):
        rows = pl.ds(pl.multiple_of(i * EP, EP), EP)
        r2, k2, v2 = rkv_ref[0, rows, :], rkv_ref[1, rows, :], rkv_ref[2, rows, :]
        y2 = y_ref[0, rows, :] + y_ref[1, rows, :]
        coef = 2.0 + (_sigmoid(al_ref[0, rows, :]) + _sigmoid(al_ref[1, rows, :]) - 2.0) * k_a
        rkr = r2 * k2 * coef * r_k
        dev = y2 - head_sum(y2, 2) * (1.0 / HEAD)
        var = head_sum(dev * dev, 2) * (1.0 / HEAD)
        z_ref[rows, :] = dev * lax.rsqrt(var + RW_GN_EPS) * lnx_w + lnx_b + head_sum(rkr, 2) * v2
        return carry

    lax.fori_loop(0, seq // EP, epilogue, 0)


def _wkvp(rkv, wl, al, s0, k_k, k_a, r_k, lnx_w, lnx_b, *, batch, seq, row_block0, emit_state):
    has_s0 = s0 is not None
    width = WKV_PAIRS * 2 * HEAD
    steps = D // width
    n_chunks = seq // CHUNK
    tok = lambda lead: pl.BlockSpec((lead, seq, width), lambda b, p: (0, row_block0 + b, p))
    vec = pl.BlockSpec((1, width), lambda b, p: (0, p))
    st = pl.BlockSpec((None, 2, WKV_PAIRS * 2, HEAD, HEAD), lambda b, p: (b, 0, p, 0, 0))
    ins, specs = [rkv, wl, al], [tok(3), tok(2), tok(2)]
    if has_s0:
        ins.append(s0)
        specs.append(st)
    ins += [k_k.reshape(1, D), k_a.reshape(1, D), r_k.reshape(1, D), lnx_w.reshape(1, D), lnx_b.reshape(1, D)]
    specs += [vec] * 5
    out_shape = [jax.ShapeDtypeStruct((batch * seq, D), F32)]
    out_specs = [pl.BlockSpec((seq, width), lambda b, p: (b, p))]
    if emit_state:
        out_shape.append(jax.ShapeDtypeStruct((batch, 2, HEADS, HEAD, HEAD), F32))
        out_specs.append(st)
    return pl.pallas_call(
        functools.partial(_wkvp_kernel, seq=seq, has_s0=has_s0, emit_state=emit_state),
        out_shape=out_shape,
        grid=(batch, steps),
        in_specs=specs,
        out_specs=out_specs,
        scratch_shapes=[pltpu.VMEM((2, WKV_PAIRS, n_chunks, 2 * HEAD, 2 * HEAD), BF16),
                        pltpu.VMEM((2, WKV_PAIRS, n_chunks, HEAD, 2 * HEAD), F32),
                        pltpu.VMEM((2, WKV_PAIRS, n_chunks, CHUNK, 2 * HEAD), BF16),
                        pltpu.VMEM((2, n_chunks, 1, width), F32),
                        pltpu.VMEM((2, seq, width), F32)],
        compiler_params=_params(("parallel", "parallel")),
        name="wkv_%d" % seq,
    )(*ins)


def _pool_kernel(x_ref, g_ref, m_ref, w_ref, sc_ref, o_ref):
    x = x_ref[...]
    h = _rms_mod(x, g_ref[...], m_ref[0], m_ref[1])
    seq = jnp.where(pl.program_id(0) < PROMPT_BLOCKS, SEQ, DEC_SEQ)
    t = lax.broadcasted_iota(jnp.int32, (ROW_BLOCK, 1), 0) & (seq - 1)
    gate = m_ref[2]
    scale = sc_ref[...]
    for g, win in enumerate(POOL_WINDOWS):
        half = win // 2
        cols = slice(g * POOL_GROUP, (g + 1) * POOL_GROUP)
        hg = h[:, cols]
        up = lambda z, m: jnp.where(t + m <= seq - 1, pltpu.roll(z, ROW_BLOCK - m, 0), 0.0)
        down = lambda z, m: jnp.where(t - m >= 0, pltpu.roll(z, m, 0), 0.0)
        fwd = hg
        bwd = down(hg, 1)
        m = 1
        while m < half:
            fwd = fwd + up(fwd, m)
            bwd = bwd + down(bwd, m)
            m *= 2
        count = (jnp.minimum(t + half - 1, seq - 1) - jnp.maximum(t - half, 0) + 1).astype(F32)
        pooled = (fwd + bwd) / count - hg
        mixed = _bdot(pooled, w_ref[g]) * scale[:, cols]
        o_ref[:, cols] = x[:, cols] + gate[:, cols] * mixed


def _pool_layer(x, gain, mods, w_pool, scale):
    row = pl.BlockSpec((ROW_BLOCK, D), lambda i: (i, 0))
    n_g = len(POOL_WINDOWS)
    return pl.pallas_call(
        _pool_kernel,
        out_shape=jax.ShapeDtypeStruct((N_TOK, D), F32),
        grid=(N_TOK // ROW_BLOCK,),
        in_specs=[row, pl.BlockSpec((1, D), lambda i: (0, 0)), _mods_spec(ROW_BLOCK),
                  pl.BlockSpec((n_g, POOL_GROUP, POOL_GROUP), lambda i: (0, 0, 0)),
                  pl.BlockSpec((1, D), lambda i: (0, 0))],
        out_specs=row,
        compiler_params=_params(("parallel",)),
        name="pool",
    )(x, gain.reshape(1, D), mods, w_pool, scale.reshape(1, D))


ATTN_PROMPT_HEADS = 8
NA_ROWS = DEC_SEQ // GRID_W
NA_WIN_R = min(NA_WIN_ROWS, NA_ROWS)


def _head_rms(x, gain):
    return x * lax.rsqrt(jnp.mean(x * x, axis=-1, keepdims=True) + NORM_EPS) * gain


def _attn_prompt_kernel(q_ref, k_ref, v_ref, qg_ref, kg_ref, o_ref, kn_ref):
    heads = [slice(hh * HEAD, (hh + 1) * HEAD) for hh in range(q_ref.shape[1] // HEAD)]
    q = [_head_rms(q_ref[:, sl], qg_ref[...]) * (HEAD ** -0.5) for sl in heads]
    k = [_head_rms(k_ref[:, sl], kg_ref[...]) for sl in heads]
    s = [_bdot_nt(qh, kh) for qh, kh in zip(q, k)]
    m = [jnp.max(sh, axis=-1, keepdims=True) for sh in s]
    p = [jnp.exp(sh - mh) for sh, mh in zip(s, m)]
    den = [jnp.sum(ph, axis=-1, keepdims=True) for ph in p]
    o = [_bdot(ph, v_ref[:, sl]) for ph, sl in zip(p, heads)]
    o_ref[...] = jnp.concatenate([oh / dh for oh, dh in zip(o, den)], axis=1)
    kn_ref[...] = jnp.concatenate(k, axis=1)


def _attn_latent_kernel(q_ref, k_ref, v_ref, qg_ref, kg_ref, ck_ref, cv_ref, bias_ref, o_ref):
    heads = [slice(hh * HEAD, (hh + 1) * HEAD) for hh in range(HEADS_PER_STEP)]
    q = [(_head_rms(q_ref[:, sl], qg_ref[...]) * (HEAD ** -0.5)).astype(BF16) for sl in heads]
    k = [_head_rms(k_ref[:, sl], kg_ref[...]).astype(BF16) for sl in heads]
    v = [v_ref[:, sl].astype(BF16) for sl in heads]
    ck = [ck_ref[:, sl].astype(BF16) for sl in heads]
    cv = [cv_ref[:, sl].astype(BF16) for sl in heads]
    units = []
    for hh in range(HEADS_PER_STEP):
        for qr in range(NA_ROWS):
            r0 = min(max(qr - NA_WIN_R // 2, 0), NA_ROWS - NA_WIN_R)
            units.append((hh, qr, r0, slice(qr * GRID_W, (qr + 1) * GRID_W), slice(r0 * GRID_W, (r0 + NA_WIN_R) * GRID_W)))
    s = [_bdot_nt(q[hh][qs], k[hh][ks])
         + jnp.concatenate([bias_ref[hh, r0 + j - qr + NA_WIN_ROWS - 1] for j in range(NA_WIN_R)], axis=1)
         for hh, qr, r0, qs, ks in units]
    s_ctx = [_bdot_nt(q[hh][qs], ck[hh]) for hh, qr, r0, qs, ks in units]
    m = [jnp.maximum(jnp.max(a, axis=-1, keepdims=True), jnp.max(b, axis=-1, keepdims=True)) for a, b in zip(s, s_ctx)]
    p = [jnp.exp(a - mm) for a, mm in zip(s, m)]
    p_ctx = [jnp.exp(b - mm) for b, mm in zip(s_ctx, m)]
    den = [jnp.sum(a, axis=-1, keepdims=True) + jnp.sum(b, axis=-1, keepdims=True) for a, b in zip(p, p_ctx)]
    o = [_bdot(a, v[hh][ks]) + _bdot(b, cv[hh]) for a, b, (hh, qr, r0, qs, ks) in zip(p, p_ctx, units)]
    o = [a / d for a, d in zip(o, den)]
    o_ref[...] = jnp.concatenate([jnp.concatenate(o[hh * NA_ROWS:(hh + 1) * NA_ROWS], axis=0)
                                  for hh in range(HEADS_PER_STEP)], axis=1)


def _attention_prompt(qkv, q_gain, k_gain):
    width = ATTN_PROMPT_HEADS * HEAD
    tok = lambda j: pl.BlockSpec((None, SEQ, width), lambda b, p: (j, b, p))
    gain = pl.BlockSpec((1, HEAD), lambda b, p: (0, 0))
    out = pl.BlockSpec((SEQ, width), lambda b, p: (b, p))
    return pl.pallas_call(
        _attn_prompt_kernel,
        out_shape=[jax.ShapeDtypeStruct((N_PROMPT, D), F32)] * 2,
        grid=(BATCH, D // width),
        in_specs=[tok(0), tok(1), tok(2), gain, gain],
        out_specs=[out, out],
        compiler_params=_params(("parallel", "parallel")),
        name="attn_prompt",
    )(qkv, qkv, qkv, q_gain.reshape(1, HEAD), k_gain.reshape(1, HEAD))


def _attention_latent(qkv, q_gain, k_gain, ck, cv, bias):
    pair = HEADS // HEADS_PER_STEP
    row0 = N_PROMPT // DEC_SEQ
    gain = pl.BlockSpec((1, HEAD), lambda p, b: (0, 0))
    tok = lambda j: pl.BlockSpec((None, DEC_SEQ, LANES), lambda p, b: (j, row0 + b, p))
    ctx = pl.BlockSpec((None, PAST_LEN, LANES), lambda p, b: (b, 0, p))
    n_dr = 2 * NA_WIN_ROWS - 1
    return pl.pallas_call(
        _attn_latent_kernel,
        out_shape=jax.ShapeDtypeStruct((N_LATENT, D), F32),
        grid=(pair, DEC_BATCH),
        in_specs=[tok(0), tok(1), tok(2), gain, gain, ctx, ctx,
                  pl.BlockSpec((HEADS_PER_STEP, n_dr, GRID_W, GRID_W), lambda p, b: (p, 0, 0, 0))],
        out_specs=pl.BlockSpec((DEC_SEQ, LANES), lambda p, b: (b, p)),
        compiler_params=_params(("parallel", "parallel")),
        name="attn_latent",
    )(qkv, qkv, qkv, q_gain.reshape(1, HEAD), k_gain.reshape(1, HEAD), ck, cv, bias)


def _column_bias_table(rpb):
    n_dr, n_dc = rpb.shape[1], rpb.shape[2]
    span = 2 * GRID_W - 1
    left = GRID_W - NA_WIN_COLS
    g = jnp.pad(rpb, ((0, 0), (0, 0), (left, span - n_dc - left)))
    flat = jnp.broadcast_to(g[:, :, None, :], (HEADS, n_dr, GRID_W, span)).reshape(HEADS, n_dr, GRID_W * span)
    table = flat[:, :, GRID_W - 1:GRID_W - 1 + GRID_W * (span - 1)].reshape(HEADS, n_dr, GRID_W, span - 1)
    table = table[..., :GRID_W]
    col = np.arange(GRID_W)
    c0 = np.clip(col - NA_WIN_COLS // 2, 0, GRID_W - NA_WIN_COLS)
    inside = (col[None, :] >= c0[:, None]) & (col[None, :] < c0[:, None] + NA_WIN_COLS)
    return jnp.where(inside[None, None], table, NEG_BIG)


ROUTER_TILE = 512


def _router_kernel(x_ref, g_ref, m_ref, w_ref, b_ref, h_ref, gate_ref):
    h = _rms_mod(x_ref[...], g_ref[...], m_ref[3], m_ref[4])
    _store_token_rows(h_ref, h)
    h_hi = h.astype(BF16)
    h_lo = (h - h_hi.astype(F32)).astype(BF16)
    w = w_ref[...]
    w_hi = w.astype(BF16)
    w_lo = (w - w_hi.astype(F32)).astype(BF16)
    logits = (jnp.dot(h_hi, w_hi, preferred_element_type=F32) + jnp.dot(h_hi, w_lo, preferred_element_type=F32)
              + jnp.dot(h_lo, w_hi, preferred_element_type=F32) + b_ref[...])
    lane = lax.broadcasted_iota(jnp.int32, logits.shape, 1)
    vals = logits
    top0 = None
    den = 0.0
    gates = jnp.full(logits.shape, -1.0, F32)
    for j in range(TOP_K):
        m = jnp.max(vals, axis=-1, keepdims=True)
        first = jnp.min(jnp.where(vals == m, lane, N_EXPERTS), axis=-1, keepdims=True)
        sel = lane == first
        if j == 0:
            top0 = m
        e = jnp.exp(m - top0)
        den = den + e
        gates = jnp.where(sel, e, gates)
        vals = jnp.where(sel, -jnp.inf, vals)
    gate_ref[...] = jnp.where(gates >= 0.0, gates / den, -1.0)


def _router(x, gain, mods, w_router, b_router):
    row = pl.BlockSpec((ROUTER_TILE, D), lambda i: (i, 0))
    return pl.pallas_call(
        _router_kernel,
        out_shape=[jax.ShapeDtypeStruct((N_TOK * ROW_TILES, LANES), F32),
                   jax.ShapeDtypeStruct((N_TOK, N_EXPERTS), F32)],
        grid=(N_TOK // ROUTER_TILE,),
        in_specs=[row, pl.BlockSpec((1, D), lambda i: (0, 0)), _mods_spec(ROUTER_TILE),
                  pl.BlockSpec((D, N_EXPERTS), lambda i: (0, 0)), pl.BlockSpec((1, N_EXPERTS), lambda i: (0, 0))],
        out_specs=[pl.BlockSpec((ROUTER_TILE * ROW_TILES, LANES), lambda i: (i, 0)),
                   pl.BlockSpec((ROUTER_TILE, N_EXPERTS), lambda i: (i, 0))],
        compiler_params=_params(("parallel",)),
        name="router",
    )(x, gain.reshape(1, D), mods, w_router, b_router.reshape(1, N_EXPERTS))


def _store_token_rows(ref, value, lead=()):
    n = value.shape[0]
    for c in range(ROW_TILES):
        ref[lead + (pl.ds(c, n, stride=ROW_TILES), slice(None))] = value[:, c * LANES:(c + 1) * LANES]


def _load_token_chunk(ref, n, c, lead=()):
    return ref[lead + (pl.ds(c, n, stride=ROW_TILES), slice(None))]


def _token_copy(src, src_tok, dst, dst_tok, sem):
    rows = lambda t: pl.ds(t * ROW_TILES if isinstance(t, int) else pl.multiple_of(t * ROW_TILES, ROW_TILES), ROW_TILES)
    return pltpu.make_async_copy(src.at[rows(src_tok)], dst.at[rows(dst_tok)], sem)


def _invert_kernel(pos_ref, pad_ref, dst_ref):
    def clear(i, carry):
        dst_ref[i] = 0
        return carry

    for e in range(N_EXPERTS):
        lax.fori_loop(pad_ref[2 * e], pad_ref[2 * e + 1], clear, 0)
    lax.fori_loop(pad_ref[2 * N_EXPERTS - 1], MOE_ROWS, clear, 0)

    def place(n, carry):
        for j in range(TOP_K):
            dst_ref[pos_ref[n * TOP_K + j]] = j * N_TOK + n
        return carry

    lax.fori_loop(0, N_TOK, place, 0, unroll=4)


def _invert(pos4, pad_bounds):
    smem = pl.BlockSpec(memory_space=pltpu.SMEM)
    return pl.pallas_call(
        _invert_kernel,
        out_shape=jax.ShapeDtypeStruct((MOE_ROWS,), jnp.int32),
        in_specs=[smem, smem],
        out_specs=smem,
        name="invert",
    )(pos4, pad_bounds)


def _experts_kernel(te_ref, tv_ref, nt_ref, dst_ref, h_hbm, wgu_ref, bgu_ref, wd_ref, bd_ref, ys_hbm,
                    xbuf, ybuf, sem_in, sem_out, wgu_bf, wd_bf, act_ref):
    t = pl.program_id(0)
    n_tiles = nt_ref[0]
    slot = t % 2
    other = 1 - slot

    def gather_row(tile, r, to_slot):
        tok = dst_ref[tile * MOE_TILE + r] & (N_TOK - 1)
        return _token_copy(h_hbm, tok, xbuf.at[to_slot], r, sem_in.at[to_slot])

    def scatter_row(tile, r, from_slot):
        return _token_copy(ybuf.at[from_slot], r, ys_hbm, dst_ref[tile * MOE_TILE + r], sem_out.at[from_slot])

    def gather_wait(s):
        pltpu.make_async_copy(h_hbm.at[pl.ds(0, MOE_TILE * ROW_TILES)], xbuf.at[s], sem_in.at[s]).wait()

    def scatter_wait(s, tokens):
        @pl.when(tokens > 0)
        def _():
            rows = pl.ds(0, pl.multiple_of(tokens * ROW_TILES, ROW_TILES))
            pltpu.make_async_copy(ybuf.at[s].at[rows], ys_hbm.at[rows], sem_out.at[s]).wait()

    @pl.when(t == 0)
    def _():
        def issue(r, carry):
            gather_row(0, r, 0).start()
            return carry

        lax.fori_loop(0, MOE_TILE, issue, 0)

    @pl.when(t < n_tiles)
    def _():
        @pl.when((t == 0) | (te_ref[t] != te_ref[jnp.maximum(t - 1, 0)]))
        def _():
            wgu_bf[...] = wgu_ref[...].astype(BF16)
            wd_bf[...] = wd_ref[...].astype(BF16)

        gather_wait(slot)
        nxt = jnp.minimum(t + 1, n_tiles - 1)
        prev, prev_rows = jnp.maximum(t - 1, 0), tv_ref[t]
        n_up, n_down = D_EXPERT // MOE_COL, D // MOE_COL

        def issue_gathers(block):
            for r in range(block * (MOE_TILE // n_up), (block + 1) * (MOE_TILE // n_up)):
                gather_row(nxt, r, other).start()

        def issue_scatters(block):
            for r in range(block * (MOE_TILE // n_down), (block + 1) * (MOE_TILE // n_down)):
                @pl.when(r < prev_rows)
                def _():
                    scatter_row(prev, r, other).start()

        x = jnp.concatenate([_load_token_chunk(xbuf, MOE_TILE, c, (slot,)) for c in range(ROW_TILES)],
                            axis=1).astype(BF16)
        for c in range(n_up):
            cols = slice(c * MOE_COL, (c + 1) * MOE_COL)
            ups = slice(D_EXPERT + c * MOE_COL, D_EXPERT + (c + 1) * MOE_COL)
            glu = jnp.dot(x, wgu_bf[:, cols], preferred_element_type=F32) + bgu_ref[:, cols]
            lin = jnp.dot(x, wgu_bf[:, ups], preferred_element_type=F32) + bgu_ref[:, ups]
            glu = jnp.minimum(glu, SWIGLU_LIMIT)
            lin = jnp.clip(lin, -SWIGLU_LIMIT, SWIGLU_LIMIT)
            act_ref[:, cols] = (glu * _sigmoid(SWIGLU_ALPHA * glu) * (lin + 1.0)).astype(BF16)
            issue_gathers(c)
        scatter_wait(slot, jnp.where(t >= 2, tv_ref[jnp.maximum(t - 1, 0)], 0))
        act = act_ref[...]
        for c in range(n_down):
            cols = slice(c * MOE_COL, (c + 1) * MOE_COL)
            y = jnp.dot(act, wd_bf[:, cols], preferred_element_type=F32) + bd_ref[:, cols]
            for k in range(MOE_COL // LANES):
                chunk = c * (MOE_COL // LANES) + k
                ybuf[slot, pl.ds(chunk, MOE_TILE, stride=ROW_TILES), :] = y[:, k * LANES:(k + 1) * LANES]
            issue_scatters(c)

    @pl.when(t == n_tiles - 1)
    def _():
        def issue(r, carry):
            scatter_row(t, r, slot).start()
            return carry

        lax.fori_loop(0, tv_ref[t + 1], issue, 0)
        gather_wait(other)
        scatter_wait(other, tv_ref[t])
        scatter_wait(slot, tv_ref[t + 1])


def _experts(h, tile_expert, tile_rows, n_tiles, dst, layer, w_gu, b_gu, w_down, b_down):
    weight = lambda shape: pl.BlockSpec((None, None) + shape, lambda t, te, tv, nt, dst: (layer, te[t], 0, 0))
    grid_spec = pltpu.PrefetchScalarGridSpec(
        num_scalar_prefetch=4,
        grid=(MOE_TILES,),
        in_specs=[pl.BlockSpec(memory_space=pl.ANY),
                  weight((D, 2 * D_EXPERT)), weight((1, 2 * D_EXPERT)), weight((D_EXPERT, D)), weight((1, D))],
        out_specs=pl.BlockSpec(memory_space=pl.ANY),
        scratch_shapes=[pltpu.VMEM((2, MOE_TILE * ROW_TILES, LANES), F32),
                        pltpu.VMEM((2, MOE_TILE * ROW_TILES, LANES), F32),
                        pltpu.SemaphoreType.DMA((2,)), pltpu.SemaphoreType.DMA((2,)),
                        pltpu.VMEM((D, 2 * D_EXPERT), BF16), pltpu.VMEM((D_EXPERT, D), BF16),
                        pltpu.VMEM((MOE_TILE, D_EXPERT), BF16)],
    )
    return pl.pallas_call(
        _experts_kernel,
        out_shape=jax.ShapeDtypeStruct((TOP_K * N_TOK * ROW_TILES, LANES), F32),
        grid_spec=grid_spec,
        compiler_params=_params(("arbitrary",)),
        name="experts",
    )(tile_expert, tile_rows, n_tiles, dst, h, w_gu, b_gu.reshape(DEPTH, N_EXPERTS, 1, 2 * D_EXPERT),
      w_down, b_down.reshape(DEPTH, N_EXPERTS, 1, D))


def _combine_kernel(y0_ref, y1_ref, y2_ref, y3_ref, gate_ref, x_ref, m_ref, o_ref):
    gate = gate_ref[...]
    g = [gate[:, j:j + 1] for j in range(TOP_K)]
    scale = m_ref[5]
    for c in range(ROW_TILES):
        cols = slice(c * LANES, (c + 1) * LANES)
        y = [_load_token_chunk(ref, COMBINE_TILE, c) for ref in (y0_ref, y1_ref, y2_ref, y3_ref)]
        moe = (y[0] * g[0] + y[1] * g[1]) + (y[2] * g[2] + y[3] * g[3])
        o_ref[:, cols] = x_ref[:, cols] + scale[:, cols] * moe


def _combine(ys, gate4, x, mods):
    row = pl.BlockSpec((COMBINE_TILE, D), lambda i: (i, 0))
    blocks = N_TOK // COMBINE_TILE
    choice = lambda j: pl.BlockSpec((COMBINE_TILE * ROW_TILES, LANES), lambda i: (j * blocks + i, 0))
    return pl.pallas_call(
        _combine_kernel,
        out_shape=jax.ShapeDtypeStruct((N_TOK, D), F32),
        grid=(blocks,),
        in_specs=[choice(j) for j in range(TOP_K)]
        + [pl.BlockSpec((COMBINE_TILE, LANES), lambda i: (i, 0)), row, _mods_spec(COMBINE_TILE)],
        out_specs=row,
        compiler_params=_params(("parallel",)),
        name="combine",
    )(ys, ys, ys, ys, gate4, x, mods)


def _moe_layer(x, gain, mods, layer, w_router, b_router, w_gu, b_gu, w_down, b_down):
    h, gates = _router(x, gain, mods, w_router, b_router)
    sel = gates >= 0.0
    sel_i = sel.astype(jnp.int32)
    rank = jnp.cumsum(sel_i, axis=0) - sel_i
    count = jnp.sum(sel_i, axis=0)
    padded = ((count + MOE_TILE - 1) // MOE_TILE) * MOE_TILE
    group_end = jnp.cumsum(padded)
    pos = group_end[None, :] - padded[None, :] + rank
    slot = jnp.cumsum(sel_i, axis=1) - 1
    pick = [sel & (slot == j) for j in range(TOP_K)]
    pos4 = jnp.stack([jnp.sum(jnp.where(m, pos, 0), axis=1) for m in pick], axis=1).astype(jnp.int32)
    gate4 = jnp.stack([jnp.sum(jnp.where(m, gates, 0.0), axis=1) for m in pick], axis=1)
    gate4 = jnp.pad(gate4, ((0, 0), (0, LANES - TOP_K)))
    pos4 = pos4.reshape(-1)
    n_tiles = (group_end[-1] // MOE_TILE).astype(jnp.int32)
    tile_start = jnp.minimum(jnp.arange(MOE_TILES, dtype=jnp.int32) * MOE_TILE, group_end[-1] - 1)
    tile_expert = jnp.sum((group_end[None, :] <= tile_start[:, None]).astype(jnp.int32), axis=1)
    tile_expert = jnp.minimum(tile_expert, N_EXPERTS - 1).astype(jnp.int32)
    real_end = group_end - padded + count
    tile_rows = jnp.clip(real_end[tile_expert] - jnp.arange(MOE_TILES, dtype=jnp.int32) * MOE_TILE, 0, MOE_TILE)
    tile_rows = jnp.concatenate([jnp.zeros((1,), jnp.int32), tile_rows.astype(jnp.int32)])
    pad_bounds = jnp.stack([real_end, group_end], axis=1).reshape(-1).astype(jnp.int32)
    dst = _invert(pos4, pad_bounds)
    ys = _experts(h, tile_expert, tile_rows, n_tiles.reshape(1), dst, layer, w_gu, b_gu, w_down, b_down)
    return _combine(ys, gate4, x, mods)


def _rwkv_layer(x, gain, mods, state, p):
    (mu, w_rkv, w_out, w0, w1, w2, a0, a1, a2, g1, g2, k_k, k_a, r_k, lnx_w, lnx_b) = p
    h, xx = _normmod(x, gain, mods, with_xx=True)
    mu = mu.reshape(N_MOD, 1, D)
    rkv = _proj3(h, w_rkv, xx, mu[jnp.array([0, 2, 3])])
    wl = _lora(h, xx, mu[jnp.array([1, 1])], w1, w2, w0.reshape(2, 1, D), "tanh")
    al = _lora(h, xx, mu[jnp.array([4, 4])], a1, a2, a0.reshape(2, 1, D), "none")
    g = _lora(h, xx, mu[5:6], g1[None], g2[None], None, "sigmoid")[0]
    vecs = (k_k, k_a, r_k, lnx_w, lnx_b)
    z_p, s_ctx = _wkvp(rkv, wl, al, None, *vecs, batch=BATCH, seq=SEQ, row_block0=0, emit_state=True)
    (z_s,) = _wkvp(rkv, wl, al, state, *vecs, batch=DEC_BATCH, seq=DEC_SEQ,
                  row_block0=N_PROMPT // DEC_SEQ, emit_state=False)
    z = jnp.concatenate([z_p, z_s], axis=0)
    return _mmres(z, w_out, x, mods, g=g), s_ctx


def _na_layer(x, gain, mods, ck, cv, p):
    w_qkv, w_out, q_norm, k_norm, rpb = p
    (h,) = _normmod(x, gain, mods, with_xx=False)
    qkv = _proj3(h, w_qkv)
    o_p, k_p = _attention_prompt(qkv, q_norm, k_norm)
    o_s = _attention_latent(qkv, q_norm, k_norm, ck.reshape(DEC_BATCH, PAST_LEN, D),
                            cv.reshape(DEC_BATCH, PAST_LEN, D), _column_bias_table(rpb))
    o = jnp.concatenate([o_p, o_s], axis=0)
    new_k = k_p.reshape(BATCH, SEQ, HEADS, HEAD)
    new_v = qkv[2, :N_PROMPT].reshape(BATCH, SEQ, HEADS, HEAD)
    return _mmres(o, w_out, x, mods), new_k, new_v


def kernel(x_prompt, x_sample, c, c_ctx, state_wkv, cache_k, cache_v, norm_mix, norm_ffn, w_mod, b_mod, rw_mu, rw_w_rkv, rw_w_out, rw_w0, rw_w1, rw_w2, rw_a0, rw_a1, rw_a2, rw_g1, rw_g2, rw_k_k, rw_k_a, rw_r_k, rw_lnx_w, rw_lnx_b, pool_w, pool_scale, na_w_qkv, na_w_out, na_q_norm, na_k_norm, na_rpb, moe_w_router, moe_b_router, moe_w_gu, moe_b_gu, moe_w_down, moe_b_down):
    x = jnp.concatenate([x_prompt.reshape(N_PROMPT, D), x_sample.reshape(N_LATENT, D)], axis=0)
    cond = jnp.concatenate([c_ctx[None, :], c, jnp.zeros((COND_ROWS - 1 - DEC_BATCH, D), F32)], axis=0)
    mods = _adaln(cond, w_mod, b_mod)
    new_wkv, new_k, new_v = [], [], []
    for i in range(DEPTH):
        kind, slot = i % N_MIXERS, i // N_MIXERS
        if kind == 0:
            rw = (rw_mu[slot], rw_w_rkv[slot], rw_w_out[slot], rw_w0[slot], rw_w1[slot], rw_w2[slot],
                  rw_a0[slot], rw_a1[slot], rw_a2[slot], rw_g1[slot], rw_g2[slot], rw_k_k[slot],
                  rw_k_a[slot], rw_r_k[slot], rw_lnx_w[slot], rw_lnx_b[slot])
            x, s_ctx = _rwkv_layer(x, norm_mix[i], mods[i], state_wkv[:, slot], rw)
            new_wkv.append(s_ctx)
        elif kind == 1:
            x = _pool_layer(x, norm_mix[i], mods[i], pool_w[slot], pool_scale[slot])
        else:
            na = (na_w_qkv[slot], na_w_out[slot], na_q_norm[slot], na_k_norm[slot], na_rpb[slot])
            x, k_p, v_p = _na_layer(x, norm_mix[i], mods[i], cache_k[:, slot], cache_v[:, slot], na)
            new_k.append(k_p)
            new_v.append(v_p)
        x = _moe_layer(x, norm_ffn[i], mods[i], i, moe_w_router[i], moe_b_router[i], moe_w_gu, moe_b_gu,
                       moe_w_down, moe_b_down)
    y_prompt = x[:N_PROMPT].reshape(BATCH, SEQ, D)
    y_sample = x[N_PROMPT:].reshape(DEC_BATCH, DEC_SEQ, D)
    return (y_prompt, y_sample, jnp.stack(new_wkv, axis=1), jnp.stack(new_k, axis=1), jnp.stack(new_v, axis=1))
```

```python
import functools

import jax
import jax.numpy as jnp
import numpy as np
from jax import lax
from jax.experimental import pallas as pl
from jax.experimental.pallas import tpu as pltpu

F32 = jnp.float32
BF16 = jnp.bfloat16

D = 1024
BATCH, SEQ = 16, 256
DEC_BATCH, DEC_SEQ = 4, 1024
DEPTH = 4
PAST_LEN = 512
GRID_W = 64
N_MIXERS = 3
N_MOD = 6
NORM_EPS = 1e-6
HEAD = 64
HEADS = D // HEAD
RW_GN_EPS = 64e-5
POOL_WINDOWS = (2, 4, 8, 16)
POOL_GROUP = D // len(POOL_WINDOWS)
NA_WIN_ROWS, NA_WIN_COLS = 8, 16
N_EXPERTS, TOP_K = 32, 4
D_EXPERT = D
SWIGLU_LIMIT, SWIGLU_ALPHA = 7.0, 1.702

N_PROMPT = BATCH * SEQ
N_LATENT = DEC_BATCH * DEC_SEQ
N_TOK = N_PROMPT + N_LATENT
COND_ROWS = 8
ROW_BLOCK = 1024
PROMPT_BLOCKS = N_PROMPT // ROW_BLOCK
LANES = 128
SUBLANES = 8
ROW_TILES = D // LANES
assert ROW_TILES == SUBLANES
HEADS_PER_STEP = LANES // HEAD
CHUNK = 64
WKV_GROUP = 4
WKV_PAIRS = 2
MOE_TILE = 256
MOE_ROWS = N_TOK * TOP_K + N_EXPERTS * MOE_TILE
MOE_TILES = MOE_ROWS // MOE_TILE
MOE_COL = 256
COMBINE_TILE = 256
NEG_BIG = -1e30
VMEM_LIMIT = 56 * 1024 * 1024


def _cond_row(block_1024):
    return jnp.maximum(block_1024 - (PROMPT_BLOCKS - 1), 0)


def _mods_spec(rows_per_block):
    per = ROW_BLOCK // rows_per_block
    return pl.BlockSpec((None, N_MOD, 1, D), lambda *ids: (_cond_row(ids[-1] // per), 0, 0, 0))


def _params(sem):
    return pltpu.CompilerParams(dimension_semantics=sem, vmem_limit_bytes=VMEM_LIMIT)


def _bdot(a, b):
    return jnp.dot(a.astype(BF16), b.astype(BF16), preferred_element_type=F32)


def _bdot_nt(a, b):
    return lax.dot_general(a.astype(BF16), b.astype(BF16), (((1,), (1,)), ((), ())),
                           preferred_element_type=F32)


def _bdot_tn(a, b):
    return lax.dot_general(a.astype(BF16), b.astype(BF16), (((0,), (0,)), ((), ())),
                           preferred_element_type=F32)


def _split3(x):
    hi = x.astype(BF16)
    r1 = x - hi.astype(F32)
    mid = r1.astype(BF16)
    lo = (r1 - mid.astype(F32)).astype(BF16)
    return hi, mid, lo


def _sigmoid(x):
    return 1.0 / (1.0 + jnp.exp(-x))


def _rms_mod(x, gain, shift, scale):
    y = x * lax.rsqrt(jnp.mean(x * x, axis=-1, keepdims=True) + NORM_EPS)
    return (y * gain) * (1.0 + scale) + shift


def _adaln_kernel(c_ref, w_ref, b_ref, o_ref):
    c = c_ref[...]
    s = c * _sigmoid(c)
    s_hi = s.astype(BF16)
    s_lo = (s - s_hi.astype(F32)).astype(BF16)
    w = w_ref[...]
    w_hi = w.astype(BF16)
    w_lo = (w - w_hi.astype(F32)).astype(BF16)
    o_ref[...] = (jnp.dot(s_hi, w_hi, preferred_element_type=F32) + jnp.dot(s_lo, w_hi, preferred_element_type=F32)
                  + jnp.dot(s_hi, w_lo, preferred_element_type=F32) + b_ref[...])


def _adaln(cond, w_mod, b_mod):
    out = pl.pallas_call(
        _adaln_kernel,
        out_shape=jax.ShapeDtypeStruct((DEPTH, COND_ROWS, N_MOD * D), F32),
        grid=(DEPTH, N_MOD),
        in_specs=[pl.BlockSpec((COND_ROWS, D), lambda l, j: (0, 0)),
                  pl.BlockSpec((None, D, D), lambda l, j: (l, 0, j)),
                  pl.BlockSpec((None, 1, D), lambda l, j: (l, 0, j))],
        out_specs=pl.BlockSpec((None, COND_ROWS, D), lambda l, j: (l, 0, j)),
        compiler_params=_params(("parallel", "parallel")),
        name="adaln",
    )(cond, w_mod, b_mod.reshape(DEPTH, 1, N_MOD * D))
    return out.reshape(DEPTH, COND_ROWS, N_MOD, 1, D)


def _normmod_kernel(x_ref, g_ref, m_ref, h_ref, *xx_ref, shift_idx, scale_idx):
    h = _rms_mod(x_ref[...], g_ref[...], m_ref[shift_idx], m_ref[scale_idx])
    h_ref[...] = h
    if xx_ref:
        seq = jnp.where(pl.program_id(0) < PROMPT_BLOCKS, SEQ, DEC_SEQ)
        t = lax.broadcasted_iota(jnp.int32, (ROW_BLOCK, 1), 0) & (seq - 1)
        prev = jnp.where(t == 0, 0.0, pltpu.roll(h, 1, 0))
        nxt = jnp.where(t == seq - 1, 0.0, pltpu.roll(h, ROW_BLOCK - 1, 0))
        xx_ref[0][...] = 0.5 * (prev + nxt) - h


def _normmod(x, gain, mods, with_xx):
    n_out = 2 if with_xx else 1
    row = pl.BlockSpec((ROW_BLOCK, D), lambda i: (i, 0))
    outs = pl.pallas_call(
        functools.partial(_normmod_kernel, shift_idx=0, scale_idx=1),
        out_shape=[jax.ShapeDtypeStruct((N_TOK, D), F32)] * n_out,
        grid=(N_TOK // ROW_BLOCK,),
        in_specs=[row, pl.BlockSpec((1, D), lambda i: (0, 0)), _mods_spec(ROW_BLOCK)],
        out_specs=[row] * n_out,
        compiler_params=_params(("parallel",)),
        name="normmod",
    )(x, gain.reshape(1, D), mods)
    return outs


PROJ_TILE = 512


def _proj3_kernel(*refs, mix):
    if mix:
        h_ref, xx_ref, mu_ref, w_ref, o_ref, wbf_ref = refs
    else:
        h_ref, w_ref, o_ref, wbf_ref = refs

    @pl.when(pl.program_id(1) == 0)
    def _():
        wbf_ref[...] = w_ref[...].astype(BF16)

    x = h_ref[...]
    if mix:
        x = x + xx_ref[...] * mu_ref[...]
    o_ref[...] = jnp.dot(x.astype(BF16), wbf_ref[...], preferred_element_type=F32)


def _proj3(h, w, xx=None, mu=None):
    mix = xx is not None
    row = pl.BlockSpec((PROJ_TILE, D), lambda j, i: (i, 0))
    if mix:
        ins = [h, xx, mu, w]
        specs = [row, row, pl.BlockSpec((None, 1, D), lambda j, i: (j, 0, 0)),
                 pl.BlockSpec((None, D, D), lambda j, i: (j, 0, 0))]
    else:
        ins = [h, w]
        specs = [row, pl.BlockSpec((D, D), lambda j, i: (0, j))]
    return pl.pallas_call(
        functools.partial(_proj3_kernel, mix=mix),
        out_shape=jax.ShapeDtypeStruct((3, N_TOK, D), F32),
        grid=(3, N_TOK // PROJ_TILE),
        in_specs=specs,
        out_specs=pl.BlockSpec((None, PROJ_TILE, D), lambda j, i: (j, i, 0)),
        scratch_shapes=[pltpu.VMEM((D, D), BF16)],
        compiler_params=_params(("arbitrary", "arbitrary")),
        name="proj3",
    )(*ins)


def _lora_kernel(h_ref, xx_ref, mu_ref, a_ref, b_ref, *rest, n, act, has_bias):
    if has_bias:
        bias_ref, o_ref = rest
    else:
        (o_ref,) = rest
    h = h_ref[...]
    xx = xx_ref[...]
    for j in range(n):
        x = h + xx * mu_ref[j]
        t = _bdot(x, a_ref[j])
        if act == "tanh":
            t = jnp.tanh(t)
        elif act == "sigmoid":
            t = _sigmoid(t)
        o = _bdot(t, b_ref[j])
        if has_bias:
            o = o + bias_ref[j]
        o_ref[j] = o


def _lora(h, xx, mu, a, b, bias, act):
    n, _, r = a.shape
    row = pl.BlockSpec((PROJ_TILE, D), lambda i: (i, 0))
    full = lambda shape: pl.BlockSpec(shape, lambda i: (0,) * len(shape))
    ins = [h, xx, mu, a, b]
    specs = [row, row, full((n, 1, D)), full((n, D, r)), full((n, r, D))]
    if bias is not None:
        ins.append(bias)
        specs.append(full((n, 1, D)))
    return pl.pallas_call(
        functools.partial(_lora_kernel, n=n, act=act, has_bias=bias is not None),
        out_shape=jax.ShapeDtypeStruct((n, N_TOK, D), F32),
        grid=(N_TOK // PROJ_TILE,),
        in_specs=specs,
        out_specs=pl.BlockSpec((n, PROJ_TILE, D), lambda i: (0, i, 0)),
        compiler_params=_params(("parallel",)),
        name="lora_" + act,
    )(*ins)


def _mmres_kernel(*refs, with_g, gate_idx):
    if with_g:
        z_ref, g_ref, w_ref, x_ref, m_ref, o_ref, wbf_ref = refs
    else:
        z_ref, w_ref, x_ref, m_ref, o_ref, wbf_ref = refs

    @pl.when(pl.program_id(0) == 0)
    def _():
        wbf_ref[...] = w_ref[...].astype(BF16)

    z = z_ref[...]
    if with_g:
        z = z * g_ref[...]
    o_ref[...] = x_ref[...] + m_ref[gate_idx] * jnp.dot(z.astype(BF16), wbf_ref[...],
                                                        preferred_element_type=F32)


def _mmres(z, w, x, mods, g=None):
    row = pl.BlockSpec((PROJ_TILE, D), lambda i: (i, 0))
    ins, specs = [z], [row]
    if g is not None:
        ins.append(g)
        specs.append(row)
    ins += [w, x, mods]
    specs += [pl.BlockSpec((D, D), lambda i: (0, 0)), row, _mods_spec(PROJ_TILE)]
    return pl.pallas_call(
        functools.partial(_mmres_kernel, with_g=g is not None, gate_idx=2),
        out_shape=jax.ShapeDtypeStruct((N_TOK, D), F32),
        grid=(N_TOK // PROJ_TILE,),
        in_specs=specs,
        out_specs=row,
        scratch_shapes=[pltpu.VMEM((D, D), BF16)],
        compiler_params=_params(("arbitrary",)),
        name="mmres",
    )(*ins)


def _block_diag(x, left):
    return jnp.concatenate([jnp.where(left, x, 0.0), jnp.where(left, 0.0, x)], axis=0)


def _wkvp_kernel(*refs, seq, has_s0, emit_state):
    it = iter(refs)
    rkv_ref, wl_ref, al_ref = next(it), next(it), next(it)
    s0_ref = next(it) if has_s0 else None
    kk_ref, ka_ref, rk_ref, lw_ref, lb_ref = next(it), next(it), next(it), next(it), next(it)
    z_ref = next(it)
    sf_ref = next(it) if emit_state else None
    w2_ref, of_ref, ry_ref, pc_ref, y_ref = next(it), next(it), next(it), next(it), next(it)
    n_chunks = seq // CHUNK
    P2 = 2 * HEAD

    ri = lax.broadcasted_iota(jnp.int32, (CHUNK, CHUNK), 0)
    ci = lax.broadcasted_iota(jnp.int32, (CHUNK, CHUNK), 1)
    tri_bf = ((ri >= ci).astype(BF16), (ri <= ci).astype(BF16))
    rp = lax.broadcasted_iota(jnp.int32, (CHUNK, P2), 0)
    lane = lax.broadcasted_iota(jnp.int32, (CHUNK, P2), 1)
    cp = lane & (HEAD - 1)
    left = lane < HEAD
    eye = (rp == cp).astype(F32)
    blk16 = (rp // 16) == (cp // 16)
    blk32 = (rp // 32) == (cp // 32)
    mask2 = (jnp.concatenate([rp > cp, rp >= cp], axis=0), jnp.concatenate([rp < cp, rp <= cp], axis=0))
    left2 = jnp.concatenate([left, left], axis=0)
    same_head = left2 == (lax.broadcasted_iota(jnp.int32, (2 * CHUNK, P2), 0) < HEAD)
    k_k, k_a, r_k = kk_ref[...], ka_ref[...], rk_ref[...]
    heads = tuple(slice(hh * HEAD, (hh + 1) * HEAD) for hh in range(WKV_PAIRS * 2))
    lanes = tuple(slice(q * P2, (q + 1) * P2) for q in range(WKV_PAIRS))
    bd = lambda x: _block_diag(x, left)

    def prepare(g, carry):
        ch = []
        for j in range(WKV_GROUP):
            cc = g * WKV_GROUP + j
            rows = pl.ds(pl.multiple_of(cc * CHUNK, CHUNK), CHUNK)
            r2, k2, v2 = rkv_ref[0, rows, :], rkv_ref[1, rows, :], rkv_ref[2, rows, :]
            kk2 = k2 * k_k
            kk2 = jnp.concatenate(
                [kk2[:, sl] / jnp.maximum(jnp.sqrt(jnp.sum(kk2[:, sl] * kk2[:, sl], axis=-1, keepdims=True)), 1e-12)
                 for sl in heads], axis=1)
            for d in range(2):
                w_in = -wl_ref[d, rows, :]
                softplus = jnp.maximum(w_in, 0.0) + jnp.log(1.0 + jnp.exp(-jnp.abs(w_in)))
                logdec = -jnp.exp(-softplus - 0.5)
                a2 = _sigmoid(al_ref[d, rows, :])
                hi, mid, lo = _split3(logdec)
                lp = (jnp.dot(tri_bf[d], hi, preferred_element_type=F32)
                      + jnp.dot(tri_bf[d], mid, preferred_element_type=F32)
                      + jnp.dot(tri_bf[d], lo, preferred_element_type=F32))
                lp_end = lp[CHUNK - 1:CHUNK, :] if d == 0 else lp[0:1, :]
                e_neg, p_end = jnp.exp(-lp), jnp.exp(lp_end)
                kt2 = kk2 * jnp.exp(lp - logdec)
                rt2 = r2 * jnp.exp(lp)
                ks2 = k2 * (1.0 + (a2 - 1.0) * k_a) * e_neg
                bs2 = kk2 * a2 * e_neg
                kh2, bh2 = ks2 * p_end, bs2 * p_end
                pc_ref[d, cc] = p_end
                for q, ql in enumerate(lanes):
                    ch.append(dict(cc=cc, rows=rows, d=d, q=q, ql=ql, kt=kt2[:, ql], rt=rt2[:, ql], ks=ks2[:, ql],
                                   bs=bs2[:, ql], kh=kh2[:, ql], bh=bh2[:, ql], v=v2[:, ql]))
        for c in ch:
            q2 = jnp.concatenate([c["kt"], c["rt"]], axis=0)
            c["a_k"] = jnp.where(mask2[c["d"]], _bdot_nt(q2, bd(c["ks"])), 0.0)
            c["a_b"] = jnp.where(mask2[c["d"]], _bdot_nt(q2, bd(c["bs"])), 0.0)
        for c in ch:
            c["av"] = _bdot(c["a_k"], bd(c["v"]))
        for c in ch:
            c["tri"] = c["a_b"][:CHUNK]
            d16 = jnp.where(blk16, c["tri"], 0.0)
            c["x"] = eye - d16
            c["p"] = _bdot(d16, bd(d16))
        for level in range(3):
            for c in ch:
                c["x"] = c["x"] + _bdot(c["x"], bd(c["p"]))
            if level < 2:
                for c in ch:
                    c["p"] = _bdot(c["p"], bd(c["p"]))
        for inner, outer in ((blk16, blk32), (blk32, None)):
            keep = (~inner) if outer is None else (outer & (~inner))
            for c in ch:
                c["t"] = _bdot(c["x"], bd(jnp.where(keep, c["tri"], 0.0)))
            for c in ch:
                c["x"] = c["x"] - _bdot(c["t"], bd(c["x"]))
        for c in ch:
            c["wu"] = _bdot(c["x"], jnp.concatenate([bd(c["kt"]), bd(c["av"][:CHUNK])], axis=1))
        for c in ch:
            wm, uv = c["wu"][:, :P2], c["wu"][:, P2:]
            corr = _bdot(c["a_b"][CHUNK:], jnp.concatenate([bd(wm), bd(uv)], axis=1))
            d, q, cc = c["d"], c["q"], c["cc"]
            ry_ref[d, q, cc] = (c["rt"] - corr[:, :P2]).astype(BF16)
            y_ref[d, c["rows"], c["ql"]] = c["av"][CHUNK:] - corr[:, P2:]
            w2_ref[d, q, cc] = jnp.where(same_head, _bdot_tn(wm, c["bh"]), 0.0).astype(BF16)
            full = _bdot_tn(jnp.concatenate([c["v"], -uv], axis=0), jnp.concatenate([c["kh"], c["bh"]], axis=0))
            of_ref[d, q, cc] = jnp.where(left, full[:HEAD], full[HEAD:])
        return carry

    lax.fori_loop(0, n_chunks // WKV_GROUP, prepare, 0)

    def advance(c, states):
        new = []
        for d in range(2):
            cc = c if d == 0 else n_chunks - 1 - c
            rows = pl.ds(pl.multiple_of(cc * CHUNK, CHUNK), CHUNK)
            p_end = pc_ref[d, cc]
            for q, ql in enumerate(lanes):
                s = states[d * WKV_PAIRS + q]
                new.append(s * p_end[:, ql] - _bdot(s, w2_ref[d, q, cc]) + of_ref[d, q, cc])
        for d in range(2):
            cc = c if d == 0 else n_chunks - 1 - c
            rows = pl.ds(pl.multiple_of(cc * CHUNK, CHUNK), CHUNK)
            for q, ql in enumerate(lanes):
                y_ref[d, rows, ql] = y_ref[d, rows, ql] + _bdot_nt(ry_ref[d, q, cc], bd(states[d * WKV_PAIRS + q]))
        return tuple(new)

    if has_s0:
        init = tuple(jnp.concatenate([s0_ref[d, 2 * q], s0_ref[d, 2 * q + 1]], axis=1)
                     for d in range(2) for q in range(WKV_PAIRS))
    else:
        init = tuple(jnp.zeros((HEAD, P2), F32) for _ in range(2 * WKV_PAIRS))
    final = lax.fori_loop(0, n_chunks, advance, init)

    if emit_state:
        for d in range(2):
            for q in range(WKV_PAIRS):
                s = final[d * WKV_PAIRS + q]
                sf_ref[d, 2 * q] = s[:, :HEAD]
                sf_ref[d, 2 * q + 1] = s[:, HEAD:]

    EP = 256
    lnx_w, lnx_b = lw_ref[...], lb_ref[...]
    width = WKV_PAIRS * P2
    hr = lax.broadcasted_iota(jnp.int32, (width, width), 0) // HEAD
    hc = lax.broadcasted_iota(jnp.int32, (width, width), 1) // HEAD
    head_ones = (hr == hc).astype(BF16)

    def head_sum(x, passes):
        hi = x.astype(BF16)
        total = jnp.dot(hi, head_ones, preferred_element_type=F32)
        if passes == 2:
            total = total + jnp.dot((x - hi.astype(F32)).astype(BF16), head_ones, preferred_element_type=F32)
        return total

    def epilogue(i, carry):
        rows = pl.ds(pl.multiple_of(i * EP, EP), EP)
        r2, k2, v2 = rkv_ref[0, rows, :], rkv_ref[1, rows, :], rkv_ref[2, rows, :]
        y2 = y_ref[0, rows, :] + y_ref[1, rows, :]
        coef = 2.0 + (_sigmoid(al_ref[0, rows, :]) + _sigmoid(al_ref[1, rows, :]) - 2.0) * k_a
        rkr = r2 * k2 * coef * r_k
        dev = y2 - head_sum(y2, 2) * (1.0 / HEAD)
        var = head_sum(dev * dev, 2) * (1.0 / HEAD)
        z_ref[rows, :] = dev * lax.rsqrt(var + RW_GN_EPS) * lnx_w + lnx_b + head_sum(rkr, 2) * v2
        return carry

    lax.fori_loop(0, seq // EP, epilogue, 0)


def _wkvp(rkv, wl, al, s0, k_k, k_a, r_k, lnx_w, lnx_b, *, batch, seq, row_block0, emit_state):
    has_s0 = s0 is not None
    width = WKV_PAIRS * 2 * HEAD
    steps = D // width
    n_chunks = seq // CHUNK
    tok = lambda lead: pl.BlockSpec((lead, seq, width), lambda b, p: (0, row_block0 + b, p))
    vec = pl.BlockSpec((1, width), lambda b, p: (0, p))
    st = pl.BlockSpec((None, 2, WKV_PAIRS * 2, HEAD, HEAD), lambda b, p: (b, 0, p, 0, 0))
    ins, specs = [rkv, wl, al], [tok(3), tok(2), tok(2)]
    if has_s0:
        ins.append(s0)
        specs.append(st)
    ins += [k_k.reshape(1, D), k_a.reshape(1, D), r_k.reshape(1, D), lnx_w.reshape(1, D), lnx_b.reshape(1, D)]
    specs += [vec] * 5
    out_shape = [jax.ShapeDtypeStruct((batch * seq, D), F32)]
    out_specs = [pl.BlockSpec((seq, width), lambda b, p: (b, p))]
    if emit_state:
        out_shape.append(jax.ShapeDtypeStruct((batch, 2, HEADS, HEAD, HEAD), F32))
        out_specs.append(st)
    return pl.pallas_call(
        functools.partial(_wkvp_kernel, seq=seq, has_s0=has_s0, emit_state=emit_state),
        out_shape=out_shape,
        grid=(batch, steps),
        in_specs=specs,
        out_specs=out_specs,
        scratch_shapes=[pltpu.VMEM((2, WKV_PAIRS, n_chunks, 2 * HEAD, 2 * HEAD), BF16),
                        pltpu.VMEM((2, WKV_PAIRS, n_chunks, HEAD, 2 * HEAD), F32),
                        pltpu.VMEM((2, WKV_PAIRS, n_chunks, CHUNK, 2 * HEAD), BF16),
                        pltpu.VMEM((2, n_chunks, 1, width), F32),
                        pltpu.VMEM((2, seq, width), F32)],
        compiler_params=_params(("parallel", "parallel")),
        name="wkv_%d" % seq,
    )(*ins)


def _pool_kernel(x_ref, g_ref, m_ref, w_ref, sc_ref, o_ref):
    x = x_ref[...]
    h = _rms_mod(x, g_ref[...], m_ref[0], m_ref[1])
    seq = jnp.where(pl.program_id(0) < PROMPT_BLOCKS, SEQ, DEC_SEQ)
    t = lax.broadcasted_iota(jnp.int32, (ROW_BLOCK, 1), 0) & (seq - 1)
    gate = m_ref[2]
    scale = sc_ref[...]
    for g, win in enumerate(POOL_WINDOWS):
        half = win // 2
        cols = slice(g * POOL_GROUP, (g + 1) * POOL_GROUP)
        hg = h[:, cols]
        up = lambda z, m: jnp.where(t + m <= seq - 1, pltpu.roll(z, ROW_BLOCK - m, 0), 0.0)
        down = lambda z, m: jnp.where(t - m >= 0, pltpu.roll(z, m, 0), 0.0)
        fwd = hg
        bwd = down(hg, 1)
        m = 1
        while m < half:
            fwd = fwd + up(fwd, m)
            bwd = bwd + down(bwd, m)
            m *= 2
        count = (jnp.minimum(t + half - 1, seq - 1) - jnp.maximum(t - half, 0) + 1).astype(F32)
        pooled = (fwd + bwd) / count - hg
        mixed = _bdot(pooled, w_ref[g]) * scale[:, cols]
        o_ref[:, cols] = x[:, cols] + gate[:, cols] * mixed


def _pool_layer(x, gain, mods, w_pool, scale):
    row = pl.BlockSpec((ROW_BLOCK, D), lambda i: (i, 0))
    n_g = len(POOL_WINDOWS)
    return pl.pallas_call(
        _pool_kernel,
        out_shape=jax.ShapeDtypeStruct((N_TOK, D), F32),
        grid=(N_TOK // ROW_BLOCK,),
        in_specs=[row, pl.BlockSpec((1, D), lambda i: (0, 0)), _mods_spec(ROW_BLOCK),
                  pl.BlockSpec((n_g, POOL_GROUP, POOL_GROUP), lambda i: (0, 0, 0)),
                  pl.BlockSpec((1, D), lambda i: (0, 0))],
        out_specs=row,
        compiler_params=_params(("parallel",)),
        name="pool",
    )(x, gain.reshape(1, D), mods, w_pool, scale.reshape(1, D))


ATTN_PROMPT_HEADS = 8
NA_ROWS = DEC_SEQ // GRID_W
NA_WIN_R = min(NA_WIN_ROWS, NA_ROWS)


def _head_rms(x, gain):
    return x * lax.rsqrt(jnp.mean(x * x, axis=-1, keepdims=True) + NORM_EPS) * gain


def _attn_prompt_kernel(q_ref, k_ref, v_ref, qg_ref, kg_ref, o_ref, kn_ref):
    heads = [slice(hh * HEAD, (hh + 1) * HEAD) for hh in range(q_ref.shape[1] // HEAD)]
    q = [_head_rms(q_ref[:, sl], qg_ref[...]) * (HEAD ** -0.5) for sl in heads]
    k = [_head_rms(k_ref[:, sl], kg_ref[...]) for sl in heads]
    s = [_bdot_nt(qh, kh) for qh, kh in zip(q, k)]
    m = [jnp.max(sh, axis=-1, keepdims=True) for sh in s]
    p = [jnp.exp(sh - mh) for sh, mh in zip(s, m)]
    den = [jnp.sum(ph, axis=-1, keepdims=True) for ph in p]
    o = [_bdot(ph, v_ref[:, sl]) for ph, sl in zip(p, heads)]
    o_ref[...] = jnp.concatenate([oh / dh for oh, dh in zip(o, den)], axis=1)
    kn_ref[...] = jnp.concatenate(k, axis=1)


def _attn_latent_kernel(q_ref, k_ref, v_ref, qg_ref, kg_ref, ck_ref, cv_ref, bias_ref, o_ref):
    heads = [slice(hh * HEAD, (hh + 1) * HEAD) for hh in range(HEADS_PER_STEP)]
    q = [(_head_rms(q_ref[:, sl], qg_ref[...]) * (HEAD ** -0.5)).astype(BF16) for sl in heads]
    k = [_head_rms(k_ref[:, sl], kg_ref[...]).astype(BF16) for sl in heads]
    v = [v_ref[:, sl].astype(BF16) for sl in heads]
    ck = [ck_ref[:, sl].astype(BF16) for sl in heads]
    cv = [cv_ref[:, sl].astype(BF16) for sl in heads]
    units = []
    for hh in range(HEADS_PER_STEP):
        for qr in range(NA_ROWS):
            r0 = min(max(qr - NA_WIN_R // 2, 0), NA_ROWS - NA_WIN_R)
            units.append((hh, qr, r0, slice(qr * GRID_W, (qr + 1) * GRID_W), slice(r0 * GRID_W, (r0 + NA_WIN_R) * GRID_W)))
    s = [_bdot_nt(q[hh][qs], k[hh][ks])
         + jnp.concatenate([bias_ref[hh, r0 + j - qr + NA_WIN_ROWS - 1] for j in range(NA_WIN_R)], axis=1)
         for hh, qr, r0, qs, ks in units]
    s_ctx = [_bdot_nt(q[hh][qs], ck[hh]) for hh, qr, r0, qs, ks in units]
    m = [jnp.maximum(jnp.max(a, axis=-1, keepdims=True), jnp.max(b, axis=-1, keepdims=True)) for a, b in zip(s, s_ctx)]
    p = [jnp.exp(a - mm) for a, mm in zip(s, m)]
    p_ctx = [jnp.exp(b - mm) for b, mm in zip(s_ctx, m)]
    den = [jnp.sum(a, axis=-1, keepdims=True) + jnp.sum(b, axis=-1, keepdims=True) for a, b in zip(p, p_ctx)]
    o = [_bdot(a, v[hh][ks]) + _bdot(b, cv[hh]) for a, b, (hh, qr, r0, qs, ks) in zip(p, p_ctx, units)]
    o = [a / d for a, d in zip(o, den)]
    o_ref[...] = jnp.concatenate([jnp.concatenate(o[hh * NA_ROWS:(hh + 1) * NA_ROWS], axis=0)
                                  for hh in range(HEADS_PER_STEP)], axis=1)


def _attention_prompt(qkv, q_gain, k_gain):
    width = ATTN_PROMPT_HEADS * HEAD
    tok = lambda j: pl.BlockSpec((None, SEQ, width), lambda b, p: (j, b, p))
    gain = pl.BlockSpec((1, HEAD), lambda b, p: (0, 0))
    out = pl.BlockSpec((SEQ, width), lambda b, p: (b, p))
    return pl.pallas_call(
        _attn_prompt_kernel,
        out_shape=[jax.ShapeDtypeStruct((N_PROMPT, D), F32)] * 2,
        grid=(BATCH, D // width),
        in_specs=[tok(0), tok(1), tok(2), gain, gain],
        out_specs=[out, out],
        compiler_params=_params(("parallel", "parallel")),
        name="attn_prompt",
    )(qkv, qkv, qkv, q_gain.reshape(1, HEAD), k_gain.reshape(1, HEAD))


def _attention_latent(qkv, q_gain, k_gain, ck, cv, bias):
    pair = HEADS // HEADS_PER_STEP
    row0 = N_PROMPT // DEC_SEQ
    gain = pl.BlockSpec((1, HEAD), lambda p, b: (0, 0))
    tok = lambda j: pl.BlockSpec((None, DEC_SEQ, LANES), lambda p, b: (j, row0 + b, p))
    ctx = pl.BlockSpec((None, PAST_LEN, LANES), lambda p, b: (b, 0, p))
    n_dr = 2 * NA_WIN_ROWS - 1
    return pl.pallas_call(
        _attn_latent_kernel,
        out_shape=jax.ShapeDtypeStruct((N_LATENT, D), F32),
        grid=(pair, DEC_BATCH),
        in_specs=[tok(0), tok(1), tok(2), gain, gain, ctx, ctx,
                  pl.BlockSpec((HEADS_PER_STEP, n_dr, GRID_W, GRID_W), lambda p, b: (p, 0, 0, 0))],
        out_specs=pl.BlockSpec((DEC_SEQ, LANES), lambda p, b: (b, p)),
        compiler_params=_params(("parallel", "parallel")),
        name="attn_latent",
    )(qkv, qkv, qkv, q_gain.reshape(1, HEAD), k_gain.reshape(1, HEAD), ck, cv, bias)


def _column_bias_table(rpb):
    n_dr, n_dc = rpb.shape[1], rpb.shape[2]
    span = 2 * GRID_W - 1
    left = GRID_W - NA_WIN_COLS
    g = jnp.pad(rpb, ((0, 0), (0, 0), (left, span - n_dc - left)))
    flat = jnp.broadcast_to(g[:, :, None, :], (HEADS, n_dr, GRID_W, span)).reshape(HEADS, n_dr, GRID_W * span)
    table = flat[:, :, GRID_W - 1:GRID_W - 1 + GRID_W * (span - 1)].reshape(HEADS, n_dr, GRID_W, span - 1)
    table = table[..., :GRID_W]
    col = np.arange(GRID_W)
    c0 = np.clip(col - NA_WIN_COLS // 2, 0, GRID_W - NA_WIN_COLS)
    inside = (col[None, :] >= c0[:, None]) & (col[None, :] < c0[:, None] + NA_WIN_COLS)
    return jnp.where(inside[None, None], table, NEG_BIG)


ROUTER_TILE = 512


def _router_kernel(x_ref, g_ref, m_ref, w_ref, b_ref, h_ref, gate_ref):
    h = _rms_mod(x_ref[...], g_ref[...], m_ref[3], m_ref[4])
    _store_token_rows(h_ref, h)
    h_hi = h.astype(BF16)
    h_lo = (h - h_hi.astype(F32)).astype(BF16)
    w = w_ref[...]
    w_hi = w.astype(BF16)
    w_lo = (w - w_hi.astype(F32)).astype(BF16)
    logits = (jnp.dot(h_hi, w_hi, preferred_element_type=F32) + jnp.dot(h_hi, w_lo, preferred_element_type=F32)
              + jnp.dot(h_lo, w_hi, preferred_element_type=F32) + b_ref[...])
    lane = lax.broadcasted_iota(jnp.int32, logits.shape, 1)
    vals = logits
    top0 = None
    den = 0.0
    gates = jnp.full(logits.shape, -1.0, F32)
    for j in range(TOP_K):
        m = jnp.max(vals, axis=-1, keepdims=True)
        first = jnp.min(jnp.where(vals == m, lane, N_EXPERTS), axis=-1, keepdims=True)
        sel = lane == first
        if j == 0:
            top0 = m
        e = jnp.exp(m - top0)
        den = den + e
        gates = jnp.where(sel, e, gates)
        vals = jnp.where(sel, -jnp.inf, vals)
    gate_ref[...] = jnp.where(gates >= 0.0, gates / den, -1.0)


def _router(x, gain, mods, w_router, b_router):
    row = pl.BlockSpec((ROUTER_TILE, D), lambda i: (i, 0))
    return pl.pallas_call(
        _router_kernel,
        out_shape=[jax.ShapeDtypeStruct((N_TOK * ROW_TILES, LANES), F32),
                   jax.ShapeDtypeStruct((N_TOK, N_EXPERTS), F32)],
        grid=(N_TOK // ROUTER_TILE,),
        in_specs=[row, pl.BlockSpec((1, D), lambda i: (0, 0)), _mods_spec(ROUTER_TILE),
                  pl.BlockSpec((D, N_EXPERTS), lambda i: (0, 0)), pl.BlockSpec((1, N_EXPERTS), lambda i: (0, 0))],
        out_specs=[pl.BlockSpec((ROUTER_TILE * ROW_TILES, LANES), lambda i: (i, 0)),
                   pl.BlockSpec((ROUTER_TILE, N_EXPERTS), lambda i: (i, 0))],
        compiler_params=_params(("parallel",)),
        name="router",
    )(x, gain.reshape(1, D), mods, w_router, b_router.reshape(1, N_EXPERTS))


def _store_token_rows(ref, value, lead=()):
    n = value.shape[0]
    for c in range(ROW_TILES):
        ref[lead + (pl.ds(c, n, stride=ROW_TILES), slice(None))] = value[:, c * LANES:(c + 1) * LANES]


def _load_token_chunk(ref, n, c, lead=()):
    return ref[lead + (pl.ds(c, n, stride=ROW_TILES), slice(None))]


def _token_copy(src, src_tok, dst, dst_tok, sem):
    rows = lambda t: pl.ds(t * ROW_TILES if isinstance(t, int) else pl.multiple_of(t * ROW_TILES, ROW_TILES), ROW_TILES)
    return pltpu.make_async_copy(src.at[rows(src_tok)], dst.at[rows(dst_tok)], sem)


def _invert_kernel(pos_ref, pad_ref, dst_ref):
    def clear(i, carry):
        dst_ref[i] = 0
        return carry

    for e in range(N_EXPERTS):
        lax.fori_loop(pad_ref[2 * e], pad_ref[2 * e + 1], clear, 0)
    lax.fori_loop(pad_ref[2 * N_EXPERTS - 1], MOE_ROWS, clear, 0)

    def place(n, carry):
        for j in range(TOP_K):
            dst_ref[pos_ref[n * TOP_K + j]] = j * N_TOK + n
        return carry

    lax.fori_loop(0, N_TOK, place, 0, unroll=4)


def _invert(pos4, pad_bounds):
    smem = pl.BlockSpec(memory_space=pltpu.SMEM)
    return pl.pallas_call(
        _invert_kernel,
        out_shape=jax.ShapeDtypeStruct((MOE_ROWS,), jnp.int32),
        in_specs=[smem, smem],
        out_specs=smem,
        name="invert",
    )(pos4, pad_bounds)


def _experts_kernel(te_ref, tv_ref, nt_ref, dst_ref, h_hbm, wgu_ref, bgu_ref, wd_ref, bd_ref, ys_hbm,
                    xbuf, ybuf, sem_in, sem_out, wgu_bf, wd_bf, act_ref):
    t = pl.program_id(0)
    n_tiles = nt_ref[0]
    slot = t % 2
    other = 1 - slot

    def gather_row(tile, r, to_slot):
        tok = dst_ref[tile * MOE_TILE + r] & (N_TOK - 1)
        return _token_copy(h_hbm, tok, xbuf.at[to_slot], r, sem_in.at[to_slot])

    def scatter_row(tile, r, from_slot):
        return _token_copy(ybuf.at[from_slot], r, ys_hbm, dst_ref[tile * MOE_TILE + r], sem_out.at[from_slot])

    def gather_wait(s):
        pltpu.make_async_copy(h_hbm.at[pl.ds(0, MOE_TILE * ROW_TILES)], xbuf.at[s], sem_in.at[s]).wait()

    def scatter_wait(s, tokens):
        @pl.when(tokens > 0)
        def _():
            rows = pl.ds(0, pl.multiple_of(tokens * ROW_TILES, ROW_TILES))
            pltpu.make_async_copy(ybuf.at[s].at[rows], ys_hbm.at[rows], sem_out.at[s]).wait()

    @pl.when(t == 0)
    def _():
        def issue(r, carry):
            gather_row(0, r, 0).start()
            return carry

        lax.fori_loop(0, MOE_TILE, issue, 0)

    @pl.when(t < n_tiles)
    def _():
        @pl.when((t == 0) | (te_ref[t] != te_ref[jnp.maximum(t - 1, 0)]))
        def _():
            wgu_bf[...] = wgu_ref[...].astype(BF16)
            wd_bf[...] = wd_ref[...].astype(BF16)

        gather_wait(slot)
        nxt = jnp.minimum(t + 1, n_tiles - 1)
        prev, prev_rows = jnp.maximum(t - 1, 0), tv_ref[t]
        n_up, n_down = D_EXPERT // MOE_COL, D // MOE_COL

        def issue_gathers(block):
            for r in range(block * (MOE_TILE // n_up), (block + 1) * (MOE_TILE // n_up)):
                gather_row(nxt, r, other).start()

        def issue_scatters(block):
            for r in range(block * (MOE_TILE // n_down), (block + 1) * (MOE_TILE // n_down)):
                @pl.when(r < prev_rows)
                def _():
                    scatter_row(prev, r, other).start(priority=r % 2)

        x = jnp.concatenate([_load_token_chunk(xbuf, MOE_TILE, c, (slot,)) for c in range(ROW_TILES)],
                            axis=1).astype(BF16)
        for c in range(n_up):
            cols = slice(c * MOE_COL, (c + 1) * MOE_COL)
            ups = slice(D_EXPERT + c * MOE_COL, D_EXPERT + (c + 1) * MOE_COL)
            glu = jnp.dot(x, wgu_bf[:, cols], preferred_element_type=F32) + bgu_ref[:, cols]
            lin = jnp.dot(x, wgu_bf[:, ups], preferred_element_type=F32) + bgu_ref[:, ups]
            glu = jnp.minimum(glu, SWIGLU_LIMIT)
            lin = jnp.clip(lin, -SWIGLU_LIMIT, SWIGLU_LIMIT)
            act_ref[:, cols] = (glu * _sigmoid(SWIGLU_ALPHA * glu) * (lin + 1.0)).astype(BF16)
            issue_gathers(c)
        scatter_wait(slot, jnp.where(t >= 2, tv_ref[jnp.maximum(t - 1, 0)], 0))
        act = act_ref[...]
        for c in range(n_down):
            cols = slice(c * MOE_COL, (c + 1) * MOE_COL)
            y = jnp.dot(act, wd_bf[:, cols], preferred_element_type=F32) + bd_ref[:, cols]
            for k in range(MOE_COL // LANES):
                chunk = c * (MOE_COL // LANES) + k
                ybuf[slot, pl.ds(chunk, MOE_TILE, stride=ROW_TILES), :] = y[:, k * LANES:(k + 1) * LANES]
            issue_scatters(c)

    @pl.when(t == n_tiles - 1)
    def _():
        def issue(r, carry):
            scatter_row(t, r, slot).start()
            return carry

        lax.fori_loop(0, tv_ref[t + 1], issue, 0)
        gather_wait(other)
        scatter_wait(other, tv_ref[t])
        scatter_wait(slot, tv_ref[t + 1])


def _experts(h, tile_expert, tile_rows, n_tiles, dst, layer, w_gu, b_gu, w_down, b_down):
    weight = lambda shape: pl.BlockSpec((None, None) + shape, lambda t, te, tv, nt, dst: (layer, te[t], 0, 0))
    grid_spec = pltpu.PrefetchScalarGridSpec(
        num_scalar_prefetch=4,
        grid=(MOE_TILES,),
        in_specs=[pl.BlockSpec(memory_space=pl.ANY),
                  weight((D, 2 * D_EXPERT)), weight((1, 2 * D_EXPERT)), weight((D_EXPERT, D)), weight((1, D))],
        out_specs=pl.BlockSpec(memory_space=pl.ANY),
        scratch_shapes=[pltpu.VMEM((2, MOE_TILE * ROW_TILES, LANES), F32),
                        pltpu.VMEM((2, MOE_TILE * ROW_TILES, LANES), F32),
                        pltpu.SemaphoreType.DMA((2,)), pltpu.SemaphoreType.DMA((2,)),
                        pltpu.VMEM((D, 2 * D_EXPERT), BF16), pltpu.VMEM((D_EXPERT, D), BF16),
                        pltpu.VMEM((MOE_TILE, D_EXPERT), BF16)],
    )
    return pl.pallas_call(
        _experts_kernel,
        out_shape=jax.ShapeDtypeStruct((TOP_K * N_TOK * ROW_TILES, LANES), F32),
        grid_spec=grid_spec,
        compiler_params=_params(("arbitrary",)),
        name="experts",
    )(tile_expert, tile_rows, n_tiles, dst, h, w_gu, b_gu.reshape(DEPTH, N_EXPERTS, 1, 2 * D_EXPERT),
      w_down, b_down.reshape(DEPTH, N_EXPERTS, 1, D))


def _combine_kernel(y0_ref, y1_ref, y2_ref, y3_ref, gate_ref, x_ref, m_ref, o_ref):
    gate = gate_ref[...]
    g = [gate[:, j:j + 1] for j in range(TOP_K)]
    scale = m_ref[5]
    for c in range(ROW_TILES):
        cols = slice(c * LANES, (c + 1) * LANES)
        y = [_load_token_chunk(ref, COMBINE_TILE, c) for ref in (y0_ref, y1_ref, y2_ref, y3_ref)]
        moe = (y[0] * g[0] + y[1] * g[1]) + (y[2] * g[2] + y[3] * g[3])
        o_ref[:, cols] = x_ref[:, cols] + scale[:, cols] * moe


def _combine(ys, gate4, x, mods):
    row = pl.BlockSpec((COMBINE_TILE, D), lambda i: (i, 0))
    blocks = N_TOK // COMBINE_TILE
    choice = lambda j: pl.BlockSpec((COMBINE_TILE * ROW_TILES, LANES), lambda i: (j * blocks + i, 0))
    return pl.pallas_call(
        _combine_kernel,
        out_shape=jax.ShapeDtypeStruct((N_TOK, D), F32),
        grid=(blocks,),
        in_specs=[choice(j) for j in range(TOP_K)]
        + [pl.BlockSpec((COMBINE_TILE, LANES), lambda i: (i, 0)), row, _mods_spec(COMBINE_TILE)],
        out_specs=row,
        compiler_params=_params(("parallel",)),
        name="combine",
    )(ys, ys, ys, ys, gate4, x, mods)


def _moe_layer(x, gain, mods, layer, w_router, b_router, w_gu, b_gu, w_down, b_down):
    h, gates = _router(x, gain, mods, w_router, b_router)
    sel = gates >= 0.0
    sel_i = sel.astype(jnp.int32)
    rank = jnp.cumsum(sel_i, axis=0) - sel_i
    count = jnp.sum(sel_i, axis=0)
    padded = ((count + MOE_TILE - 1) // MOE_TILE) * MOE_TILE
    group_end = jnp.cumsum(padded)
    pos = group_end[None, :] - padded[None, :] + rank
    slot = jnp.cumsum(sel_i, axis=1) - 1
    pick = [sel & (slot == j) for j in range(TOP_K)]
    pos4 = jnp.stack([jnp.sum(jnp.where(m, pos, 0), axis=1) for m in pick], axis=1).astype(jnp.int32)
    gate4 = jnp.stack([jnp.sum(jnp.where(m, gates, 0.0), axis=1) for m in pick], axis=1)
    gate4 = jnp.pad(gate4, ((0, 0), (0, LANES - TOP_K)))
    pos4 = pos4.reshape(-1)
    n_tiles = (group_end[-1] // MOE_TILE).astype(jnp.int32)
    tile_start = jnp.minimum(jnp.arange(MOE_TILES, dtype=jnp.int32) * MOE_TILE, group_end[-1] - 1)
    tile_expert = jnp.sum((group_end[None, :] <= tile_start[:, None]).astype(jnp.int32), axis=1)
    tile_expert = jnp.minimum(tile_expert, N_EXPERTS - 1).astype(jnp.int32)
    real_end = group_end - padded + count
    tile_rows = jnp.clip(real_end[tile_expert] - jnp.arange(MOE_TILES, dtype=jnp.int32) * MOE_TILE, 0, MOE_TILE)
    tile_rows = jnp.concatenate([jnp.zeros((1,), jnp.int32), tile_rows.astype(jnp.int32)])
    pad_bounds = jnp.stack([real_end, group_end], axis=1).reshape(-1).astype(jnp.int32)
    dst = _invert(pos4, pad_bounds)
    ys = _experts(h, tile_expert, tile_rows, n_tiles.reshape(1), dst, layer, w_gu, b_gu, w_down, b_down)
    return _combine(ys, gate4, x, mods)


def _rwkv_layer(x, gain, mods, state, p):
    (mu, w_rkv, w_out, w0, w1, w2, a0, a1, a2, g1, g2, k_k, k_a, r_k, lnx_w, lnx_b) = p
    h, xx = _normmod(x, gain, mods, with_xx=True)
    mu = mu.reshape(N_MOD, 1, D)
    rkv = _proj3(h, w_rkv, xx, mu[jnp.array([0, 2, 3])])
    wl = _lora(h, xx, mu[jnp.array([1, 1])], w1, w2, w0.reshape(2, 1, D), "tanh")
    al = _lora(h, xx, mu[jnp.array([4, 4])], a1, a2, a0.reshape(2, 1, D), "none")
    g = _lora(h, xx, mu[5:6], g1[None], g2[None], None, "sigmoid")[0]
    vecs = (k_k, k_a, r_k, lnx_w, lnx_b)
    z_p, s_ctx = _wkvp(rkv, wl, al, None, *vecs, batch=BATCH, seq=SEQ, row_block0=0, emit_state=True)
    (z_s,) = _wkvp(rkv, wl, al, state, *vecs, batch=DEC_BATCH, seq=DEC_SEQ,
                  row_block0=N_PROMPT // DEC_SEQ, emit_state=False)
    z = jnp.concatenate([z_p, z_s], axis=0)
    return _mmres(z, w_out, x, mods, g=g), s_ctx


def _na_layer(x, gain, mods, ck, cv, p):
    w_qkv, w_out, q_norm, k_norm, rpb = p
    (h,) = _normmod(x, gain, mods, with_xx=False)
    qkv = _proj3(h, w_qkv)
    o_p, k_p = _attention_prompt(qkv, q_norm, k_norm)
    o_s = _attention_latent(qkv, q_norm, k_norm, ck.reshape(DEC_BATCH, PAST_LEN, D),
                            cv.reshape(DEC_BATCH, PAST_LEN, D), _column_bias_table(rpb))
    o = jnp.concatenate([o_p, o_s], axis=0)
    new_k = k_p.reshape(BATCH, SEQ, HEADS, HEAD)
    new_v = qkv[2, :N_PROMPT].reshape(BATCH, SEQ, HEADS, HEAD)
    return _mmres(o, w_out, x, mods), new_k, new_v


def kernel(x_prompt, x_sample, c, c_ctx, state_wkv, cache_k, cache_v, norm_mix, norm_ffn, w_mod, b_mod, rw_mu, rw_w_rkv, rw_w_out, rw_w0, rw_w1, rw_w2, rw_a0, rw_a1, rw_a2, rw_g1, rw_g2, rw_k_k, rw_k_a, rw_r_k, rw_lnx_w, rw_lnx_b, pool_w, pool_scale, na_w_qkv, na_w_out, na_q_norm, na_k_norm, na_rpb, moe_w_router, moe_b_router, moe_w_gu, moe_b_gu, moe_w_down, moe_b_down):
    x = jnp.concatenate([x_prompt.reshape(N_PROMPT, D), x_sample.reshape(N_LATENT, D)], axis=0)
    cond = jnp.concatenate([c_ctx[None, :], c, jnp.zeros((COND_ROWS - 1 - DEC_BATCH, D), F32)], axis=0)
    mods = _adaln(cond, w_mod, b_mod)
    new_wkv, new_k, new_v = [], [], []
    for i in range(DEPTH):
        kind, slot = i % N_MIXERS, i // N_MIXERS
        if kind == 0:
            rw = (rw_mu[slot], rw_w_rkv[slot], rw_w_out[slot], rw_w0[slot], rw_w1[slot], rw_w2[slot],
                  rw_a0[slot], rw_a1[slot], rw_a2[slot], rw_g1[slot], rw_g2[slot], rw_k_k[slot],
                  rw_k_a[slot], rw_r_k[slot], rw_lnx_w[slot], rw_lnx_b[slot])
            x, s_ctx = _rwkv_layer(x, norm_mix[i], mods[i], state_wkv[:, slot], rw)
            new_wkv.append(s_ctx)
        elif kind == 1:
            x = _pool_layer(x, norm_mix[i], mods[i], pool_w[slot], pool_scale[slot])
        else:
            na = (na_w_qkv[slot], na_w_out[slot], na_q_norm[slot], na_k_norm[slot], na_rpb[slot])
            x, k_p, v_p = _na_layer(x, norm_mix[i], mods[i], cache_k[:, slot], cache_v[:, slot], na)
            new_k.append(k_p)
            new_v.append(v_p)
        x = _moe_layer(x, norm_ffn[i], mods[i], i, moe_w_router[i], moe_b_router[i], moe_w_gu, moe_b_gu,
                       moe_w_down, moe_b_down)
    y_prompt = x[:N_PROMPT].reshape(BATCH, SEQ, D)
    y_sample = x[N_PROMPT:].reshape(DEC_BATCH, DEC_SEQ, D)
    return (y_prompt, y_sample, jnp.stack(new_wkv, axis=1), jnp.stack(new_k, axis=1), jnp.stack(new_v, axis=1))
```
